```python
import jax, jax.numpy as jnp
from jax import lax
import numpy as np

D_MODEL = 1024
BATCH = 8
SEQ = 2048
DEPTH = 4
DEC_BATCH = 128
DEC_SEQ = 8
PAST_LEN = 16384
PAGE_SIZE = 128

LRU_WIDTH = D_MODEL
LRU_HEADS = 16
LRU_HEAD_DIM = LRU_WIDTH // LRU_HEADS
CONV_W = 4
LRU_C = 8.0
RWKV_WIDTH = D_MODEL
RWKV_HEAD = 64
RWKV_HEADS = RWKV_WIDTH // RWKV_HEAD
DECAY_RANK = 64
AAA_RANK = 64
GATE_RANK = 128
SHIFT_WIDTH = 3 * RWKV_WIDTH + DECAY_RANK + AAA_RANK + GATE_RANK
IN_COLS = 2 * LRU_WIDTH + SHIFT_WIDTH + 2 * D_MODEL
D_FF = 4 * D_MODEL
DN_ALPHA = (2 * DEPTH) ** 0.25
DN_BETA = (8 * DEPTH) ** -0.25
LN_EPS = 1e-5
GN_EPS = 64e-5

kernel_name = "hawk_rwkv7_parallel_deepnorm_step"


def _layer_norm(x, g, b):
    xf = x.astype(jnp.float32)
    mu = jnp.mean(xf, -1, keepdims=True)
    var = jnp.mean(jnp.square(xf - mu), -1, keepdims=True)
    y = (xf - mu) * lax.rsqrt(var + LN_EPS) * g.astype(jnp.float32) + b.astype(jnp.float32)
    return y.astype(x.dtype)


def _rg_lru(u, h0, pos, wa, ba, wx, bx, a_param):
    bsz, t, c = u.shape
    f32 = jnp.float32
    uf = u.astype(f32)
    uh = uf.reshape(bsz, t, LRU_HEADS, LRU_HEAD_DIM)
    gate_r = jax.nn.sigmoid(jnp.einsum("bthi,hij->bthj", uh, wa.astype(f32)) + ba.astype(f32)).reshape(bsz, t, c)
    gate_i = jax.nn.sigmoid(jnp.einsum("bthi,hij->bthj", uh, wx.astype(f32)) + bx.astype(f32)).reshape(bsz, t, c)
    log_a = -LRU_C * gate_r * jax.nn.softplus(a_param.astype(f32))
    a = jnp.exp(log_a)
    mult = jnp.sqrt(-jnp.expm1(2.0 * log_a))
    mult = jnp.where((pos == 0)[None, :, None], 1.0, mult)
    xin = uf * gate_i * mult

    def combine(left, right):
        a_l, b_l = left
        a_r, b_r = right
        return a_l * a_r, a_r * b_l + b_r

    a_cum, h_from_zero = lax.associative_scan(combine, (a, xin), axis=1)
    h = a_cum * h0.astype(f32)[:, None] + h_from_zero
    return h, h[:, -1]


def _wkv7(r, decay, k, v, a_vec, b_vec, s0):
    def step(s, inp):
        r_t, d_t, k_t, v_t, a_t, b_t = inp
        sa = jnp.einsum("bhvk,bhk->bhv", s, a_t)
        s = s * d_t[:, :, None, :] + sa[..., None] * b_t[:, :, None, :] + v_t[..., None] * k_t[:, :, None, :]
        return s, jnp.einsum("bhvk,bhk->bhv", s, r_t)

    seq = tuple(jnp.swapaxes(z, 0, 1) for z in (r, decay, k, v, a_vec, b_vec))
    s_last, out = lax.scan(step, s0, seq)
    return jnp.swapaxes(out, 0, 1), s_last


def _layer(x, conv_buf, h0, shift_prev, s0, pos, lp):
    bsz, t, _ = x.shape
    dt = x.dtype
    f32 = jnp.float32
    proj = x @ lp["w_in"]
    lru_x, lru_y, shifted, gates = jnp.split(
        proj, [LRU_WIDTH, 2 * LRU_WIDTH, 2 * LRU_WIDTH + SHIFT_WIDTH], axis=-1)

    conv_in = jnp.concatenate([conv_buf.astype(dt), lru_x], axis=1)
    u = lp["conv_b"] + sum(conv_in[:, j:j + t] * lp["conv_w"][j] for j in range(CONV_W))
    new_conv = conv_in[:, -(CONV_W - 1):]
    h, h_last = _rg_lru(u, h0, pos, lp["lru_wa"], lp["lru_ba"], lp["lru_wx"], lp["lru_bx"], lp["lru_a_param"])
    out_a = h.astype(dt) * jax.nn.gelu(lru_y)

    prev = jnp.concatenate([shift_prev[:, None].astype(dt), shifted[:, :-1]], axis=1)
    mixed = shifted + (prev - shifted) * lp["shift_mu"]
    new_shift = shifted[:, -1]
    rw = RWKV_WIDTH
    r, k, v, xw, xa, xg = jnp.split(
        mixed, [rw, 2 * rw, 3 * rw, 3 * rw + DECAY_RANK, 3 * rw + DECAY_RANK + AAA_RANK], axis=-1)
    w_log = -jax.nn.softplus(-(lp["w0"] + jnp.tanh(xw) @ lp["decay_up"]).astype(f32)) - 0.5
    decay = jnp.exp(-jnp.exp(w_log))
    a = jax.nn.sigmoid((lp["a0"] + xa @ lp["aaa_up"]).astype(f32))
    g = jax.nn.sigmoid(xg) @ lp["gate_up"]

    def heads(z):
        return z.astype(f32).reshape(bsz, t, RWKV_HEADS, RWKV_HEAD)

    kk = heads(k * lp["k_k"])
    kk = kk / jnp.maximum(jnp.sqrt(jnp.sum(jnp.square(kk), -1, keepdims=True)), 1e-12)
    a_h = heads(a)
    k_a = lp["k_a"].astype(f32).reshape(RWKV_HEADS, RWKV_HEAD)
    k_h = heads(k) * (1.0 + (a_h - 1.0) * k_a)
    r_h = heads(r)
    v_h = heads(v)
    o, s_last = _wkv7(r_h, heads(decay), k_h, v_h, -kk, kk * a_h, s0.astype(f32))
    o_mu = jnp.mean(o, -1, keepdims=True)
    o_var = jnp.mean(jnp.square(o - o_mu), -1, keepdims=True)
    o_n = ((o - o_mu) * lax.rsqrt(o_var + GN_EPS)).reshape(bsz, t, rw)
    o_n = o_n * lp["gn_w"].astype(f32) + lp["gn_b"].astype(f32)
    bonus = jnp.sum(r_h * k_h * lp["r_k"].astype(f32), -1, keepdims=True) * v_h
    out_b = ((o_n + bonus.reshape(bsz, t, rw)) * g.astype(f32)).astype(dt)

    gate_a, gate_b = jnp.split(gates, 2, axis=-1)
    merged = jax.nn.sigmoid(gate_a) * out_a + jax.nn.sigmoid(gate_b) * out_b
    x = _layer_norm(DN_ALPHA * x + merged @ lp["w_out"], lp["ln1_g"], lp["ln1_b"])

    hid = jnp.square(jax.nn.relu(x @ lp["mlp_w1"]))
    x = _layer_norm(DN_ALPHA * x + hid @ lp["mlp_w2"], lp["ln2_g"], lp["ln2_b"])
    return x, new_conv, h_last, new_shift, s_last


def setup_inputs(seed: int = 0) -> dict:
    key = jax.random.key(seed)
    ks = jax.random.split(key, 40)

    def nrm(k, shape, scale):
        return jax.random.normal(k, shape, jnp.float32) * scale

    L = DEPTH
    unif = jax.random.uniform(ks[10], (L, LRU_WIDTH), jnp.float32, 0.9 ** 2, 0.999 ** 2)
    a_real = 0.5 * jnp.log(unif)
    lru_a_param = jnp.log(jnp.expm1(-a_real))
    return {
        "x_prompt": nrm(ks[0], (BATCH, SEQ, D_MODEL), 1.0),
        "x_sample": nrm(ks[1], (DEC_BATCH, DEC_SEQ, D_MODEL), 1.0),
        "state_conv": nrm(ks[2], (L, DEC_BATCH, CONV_W - 1, LRU_WIDTH), 1.0),
        "state_lru": nrm(ks[3], (L, DEC_BATCH, LRU_WIDTH), 0.5),
        "state_shift": nrm(ks[4], (L, DEC_BATCH, SHIFT_WIDTH), 1.0),
        "state_wkv": nrm(ks[5], (L, DEC_BATCH, RWKV_HEADS, RWKV_HEAD, RWKV_HEAD), 0.1),
        "w_in": nrm(ks[6], (L, D_MODEL, IN_COLS), D_MODEL ** -0.5),
        "conv_w": nrm(ks[7], (L, CONV_W, LRU_WIDTH), 0.5),
        "conv_b": nrm(ks[8], (L, LRU_WIDTH), 0.01),
        "lru_wa": nrm(ks[9], (L, LRU_HEADS, LRU_HEAD_DIM, LRU_HEAD_DIM), LRU_HEAD_DIM ** -0.5),
        "lru_ba": nrm(ks[11], (L, LRU_HEADS, LRU_HEAD_DIM), 0.01),
        "lru_wx": nrm(ks[12], (L, LRU_HEADS, LRU_HEAD_DIM, LRU_HEAD_DIM), LRU_HEAD_DIM ** -0.5),
        "lru_bx": nrm(ks[13], (L, LRU_HEADS, LRU_HEAD_DIM), 0.01),
        "lru_a_param": lru_a_param,
        "shift_mu": jax.random.uniform(ks[14], (L, SHIFT_WIDTH), jnp.float32),
        "decay_up": nrm(ks[15], (L, DECAY_RANK, RWKV_WIDTH), 0.5 * DECAY_RANK ** -0.5),
        "w0": jax.random.uniform(ks[16], (L, RWKV_WIDTH), jnp.float32, -3.0, 1.0),
        "aaa_up": nrm(ks[17], (L, AAA_RANK, RWKV_WIDTH), 0.5 * AAA_RANK ** -0.5),
        "a0": nrm(ks[18], (L, RWKV_WIDTH), 0.1),
        "gate_up": nrm(ks[19], (L, GATE_RANK, RWKV_WIDTH), GATE_RANK ** -0.5),
        "k_k": 0.85 + nrm(ks[20], (L, RWKV_WIDTH), 0.02),
        "k_a": 1.0 + nrm(ks[21], (L, RWKV_WIDTH), 0.02),
        "r_k": nrm(ks[22], (L, RWKV_HEADS, RWKV_HEAD), 0.1),
        "gn_w": 1.0 + nrm(ks[23], (L, RWKV_WIDTH), 0.02),
        "gn_b": nrm(ks[24], (L, RWKV_WIDTH), 0.01),
        "w_out": nrm(ks[25], (L, D_MODEL, D_MODEL), DN_BETA * D_MODEL ** -0.5),
        "ln1_g": 1.0 + nrm(ks[26], (L, D_MODEL), 0.02),
        "ln1_b": nrm(ks[27], (L, D_MODEL), 0.01),
        "mlp_w1": nrm(ks[28], (L, D_MODEL, D_FF), D_MODEL ** -0.5),
        "mlp_w2": nrm(ks[29], (L, D_FF, D_MODEL), DN_BETA * D_FF ** -0.5),
        "ln2_g": 1.0 + nrm(ks[30], (L, D_MODEL), 0.02),
        "ln2_b": nrm(ks[31], (L, D_MODEL), 0.01),
    }


def reference(x_prompt, x_sample, state_conv, state_lru, state_shift, state_wkv,
              w_in, conv_w, conv_b, lru_wa, lru_ba, lru_wx, lru_bx, lru_a_param,
              shift_mu, decay_up, w0, aaa_up, a0, gate_up, k_k, k_a, r_k, gn_w, gn_b,
              w_out, ln1_g, ln1_b, mlp_w1, mlp_w2, ln2_g, ln2_b):
    bp, tp, _ = x_prompt.shape
    ts = x_sample.shape[1]
    pos_p = jnp.arange(tp)
    pos_s = PAST_LEN + jnp.arange(ts)
    dt = x_prompt.dtype
    zc = jnp.zeros((bp, CONV_W - 1, LRU_WIDTH), dt)
    zh = jnp.zeros((bp, LRU_WIDTH), jnp.float32)
    zsh = jnp.zeros((bp, SHIFT_WIDTH), dt)
    zs = jnp.zeros((bp, RWKV_HEADS, RWKV_HEAD, RWKV_HEAD), jnp.float32)

    xp, xs = x_prompt, x_sample
    conv_p, lru_p, shift_p, wkv_p = [], [], [], []
    conv_s, lru_s, shift_s, wkv_s = [], [], [], []
    for l in range(DEPTH):
        lp = {
            "w_in": w_in[l], "conv_w": conv_w[l], "conv_b": conv_b[l],
            "lru_wa": lru_wa[l], "lru_ba": lru_ba[l], "lru_wx": lru_wx[l], "lru_bx": lru_bx[l],
            "lru_a_param": lru_a_param[l], "shift_mu": shift_mu[l],
            "decay_up": decay_up[l], "w0": w0[l], "aaa_up": aaa_up[l], "a0": a0[l],
            "gate_up": gate_up[l], "k_k": k_k[l], "k_a": k_a[l], "r_k": r_k[l],
            "gn_w": gn_w[l], "gn_b": gn_b[l], "w_out": w_out[l],
            "ln1_g": ln1_g[l], "ln1_b": ln1_b[l], "mlp_w1": mlp_w1[l], "mlp_w2": mlp_w2[l],
            "ln2_g": ln2_g[l], "ln2_b": ln2_b[l],
        }
        xp, c, h, sh, s = _layer(xp, zc, zh, zsh, zs, pos_p, lp)
        conv_p.append(c); lru_p.append(h); shift_p.append(sh); wkv_p.append(s)
        xs, c, h, sh, s = _layer(xs, state_conv[l], state_lru[l], state_shift[l], state_wkv[l], pos_s, lp)
        conv_s.append(c); lru_s.append(h); shift_s.append(sh); wkv_s.append(s)

    new_conv_p = jnp.stack(conv_p).astype(state_conv.dtype)
    new_lru_p = jnp.stack(lru_p).astype(state_lru.dtype)
    new_shift_p = jnp.stack(shift_p).astype(state_shift.dtype)
    new_wkv_p = jnp.stack(wkv_p).astype(state_wkv.dtype)
    new_conv_s = jnp.stack(conv_s).astype(state_conv.dtype)
    new_lru_s = jnp.stack(lru_s).astype(state_lru.dtype)
    new_shift_s = jnp.stack(shift_s).astype(state_shift.dtype)
    new_wkv_s = jnp.stack(wkv_s).astype(state_wkv.dtype)
    return (xp, xs, new_conv_p, new_lru_p, new_shift_p, new_wkv_p,
            new_conv_s, new_lru_s, new_shift_s, new_wkv_s)
```

```python
import functools
import math

import jax
import jax.numpy as jnp
from jax import lax
from jax.experimental import pallas as pl
from jax.experimental.pallas import tpu as pltpu

F32 = jnp.float32
BF16 = jnp.bfloat16

D_MODEL = 1024
DEPTH = 4
LRU_C = 8.0
HEAD = 64
GROUP = 256
N_GROUPS = D_MODEL // GROUP
SHIFT_WIDTH = 3 * D_MODEL + 64 + 64 + 128
D_FF = 4 * D_MODEL
DN_ALPHA = (2 * DEPTH) ** 0.25
LN_EPS = 1e-5
GN_EPS = 64e-5
CHUNK = 64
VMEM_LIMIT = 56 * 1024 * 1024


def _split(x, n):
  pieces = []
  rem = x
  for i in range(n):
    p = rem.astype(BF16)
    pieces.append(p)
    if i + 1 < n:
      rem = rem - p.astype(F32)
  return pieces


def _dot(a, b, trans_b=False):
  dims = (((1,), (1,)), ((), ())) if trans_b else (((1,), (0,)), ((), ()))
  return lax.dot_general(a, b, dims, preferred_element_type=F32)


def _mm(a, b, na=1, nb=1, trans_b=False):
  ap = [a] if a.dtype == BF16 else _split(a, na)
  bp = [b] if b.dtype == BF16 else _split(b, nb)
  order = max(len(ap), len(bp))
  acc = None
  for i, x in enumerate(ap):
    for j, y in enumerate(bp):
      if i + j < order:
        t = _dot(x, y, trans_b)
        acc = t if acc is None else acc + t
  return acc


def _softplus(x):
  return jnp.maximum(x, 0.0) + jnp.log1p(jnp.exp(-jnp.abs(x)))


def _expm1(x):
  u = jnp.exp(x)
  lu = jnp.log(u)
  small = jnp.where(u == 1.0, x, (u - 1.0) * x / jnp.where(u == 1.0, 1.0, lu))
  return jnp.where(jnp.abs(x) < 0.5, small, u - 1.0)


def _gelu_tanh(x):
  c = math.sqrt(2.0 / math.pi)
  return x * (0.5 * (1.0 + jnp.tanh(c * (x + 0.044715 * (x * x * x)))))


def _layer_norm(x, g, b):
  mu = jnp.mean(x, axis=-1, keepdims=True)
  xc = x - mu
  var = jnp.mean(xc * xc, axis=-1, keepdims=True)
  return xc * lax.rsqrt(var + LN_EPS) * g + b


def _iota(shape, dim):
  return lax.broadcasted_iota(jnp.int32, shape, dim)


def _tile4(x):
  return jnp.concatenate([x, x, x, x], axis=0)


def _inproj_kernel(x_ref, w_ref, o_ref):
  o_ref[...] = _dot(x_ref[...].astype(BF16), w_ref[...])


def _inproj(x, w, l, tm=512):
  n = x.shape[0]
  width = w.shape[2]
  return pl.pallas_call(
      _inproj_kernel,
      grid=(n // tm,),
      in_specs=[
          pl.BlockSpec((tm, D_MODEL), lambda i: (i, 0)),
          pl.BlockSpec((None, D_MODEL, width), lambda i: (l, 0, 0)),
      ],
      out_specs=pl.BlockSpec((tm, width), lambda i: (i, 0)),
      out_shape=jax.ShapeDtypeStruct((n, width), F32),
      compiler_params=pltpu.CompilerParams(
          dimension_semantics=("arbitrary",), vmem_limit_bytes=VMEM_LIMIT),
      name="inproj",
  )(x, w)


def _lru_kernel(chain, *refs):
  if chain:
    (x_ref, y_ref, ga_ref, cw_ref, cb_ref, wa_ref, ba_ref, wx_ref, bx_ref, ap_ref,
     pa_ref, hl_ref, cx_scr, ch_scr) = refs
  else:
    (x_ref, y_ref, ga_ref, prev_ref, h0_ref, cw_ref, cb_ref, wa_ref, ba_ref, wx_ref, bx_ref,
     ap_ref, pa_ref, hl_ref) = refs
  rows = x_ref.shape[0]
  nt = rows // 8
  x = x_ref[...]
  x3 = x.reshape(nt, 8, D_MODEL)
  if chain:
    c = pl.program_id(1)

    @pl.when(c == 0)
    def _():
      cx_scr[...] = jnp.zeros_like(cx_scr)
      ch_scr[...] = jnp.zeros_like(ch_scr)

    prev3 = jnp.concatenate([cx_scr[...].reshape(1, 8, D_MODEL), x3[:nt - 1]], axis=0)
    cx_scr[...] = x_ref[rows - 8:rows, :]
  else:
    prev3 = prev_ref[...].reshape(nt, 8, D_MODEL)

  sub = _iota((nt, 8, D_MODEL), 1)
  u = cb_ref[...] + cw_ref[3:4, :] * x
  for j in (1, 2, 3):
    xs = jnp.where(sub >= j, pltpu.roll(x3, j, 1), pltpu.roll(prev3, j, 1))
    u = u + cw_ref[3 - j:4 - j, :] * xs.reshape(rows, D_MODEL)

  gr, gi = [], []
  for g in range(N_GROUPS):
    ug = u[:, g * GROUP:(g + 1) * GROUP].astype(BF16)
    gr.append(_dot(ug, wa_ref[g]))
    gi.append(_dot(ug, wx_ref[g]))
  gate_r = jax.nn.sigmoid(jnp.concatenate(gr, axis=1) + ba_ref[...])
  gate_i = jax.nn.sigmoid(jnp.concatenate(gi, axis=1) + bx_ref[...])
  log_a = (-LRU_C * gate_r) * _softplus(ap_ref[...])
  a = jnp.exp(log_a)
  mult = jnp.sqrt(-_expm1(2.0 * log_a))
  if chain:
    first = jnp.logical_and(_iota((rows, D_MODEL), 0) == 0, c == 0)
    mult = jnp.where(first, 1.0, mult)
  xin = u * gate_i * mult

  a3 = a.reshape(nt, 8, D_MODEL)
  b3 = xin.reshape(nt, 8, D_MODEL)
  for s in (1, 2, 4):
    m = sub >= s
    b3 = jnp.where(m, a3 * pltpu.roll(b3, s, 1) + b3, b3)
    a3 = jnp.where(m, a3 * pltpu.roll(a3, s, 1), a3)
  if chain:
    hc = ch_scr[7:8, :]
    hs = []
    for i in range(nt):
      hi = a3[i] * hc + b3[i]
      hs.append(hi)
      hc = hi[7:8, :]
    h = jnp.concatenate(hs, axis=0)
    ch_scr[...] = hs[-1]
    hl_ref[0] = hc
  else:
    h3 = a3 * h0_ref[...] + b3
    hl_ref[...] = h3[:, 7:8, :]
    h = h3.reshape(rows, D_MODEL)

  pa_ref[...] = jax.nn.sigmoid(ga_ref[...]) * (h * _gelu_tanh(y_ref[...]))


def _lru(xy, gates, prm, l, *, chain, nseq, tlen, row0, prev=None, h0=None):
  if chain:
    rows = 256
    grid = (nseq, tlen // rows)
    blk0 = row0 // rows
    rmap = lambda b, c: (blk0 + b * (tlen // rows) + c, 0)
    hl_spec = pl.BlockSpec((1, 1, D_MODEL), lambda b, c: (b, 0, 0))
  else:
    nb = 16
    rows = nb * tlen
    grid = (nseq // nb, 1)
    blk0 = row0 // rows
    rmap = lambda b, c: (blk0 + b, 0)
    hl_spec = pl.BlockSpec((nb, 1, D_MODEL), lambda b, c: (b, 0, 0))
  rmap1 = lambda b, c: (rmap(b, c)[0], 1)
  vec = lambda arr: pl.BlockSpec((None, 1, D_MODEL), lambda b, c: (l, 0, 0))
  in_specs = [pl.BlockSpec((rows, D_MODEL), rmap),
              pl.BlockSpec((rows, D_MODEL), rmap1),
              pl.BlockSpec((rows, D_MODEL), rmap)]
  args = [xy, xy, gates]
  if not chain:
    in_specs += [pl.BlockSpec((rows, D_MODEL), lambda b, c: (b, 0)),
                 pl.BlockSpec((nb, 1, D_MODEL), lambda b, c: (b, 0, 0))]
    args += [prev, h0]
  in_specs += [pl.BlockSpec((None, 4, D_MODEL), lambda b, c: (l, 0, 0)),
               vec(None),
               pl.BlockSpec((None, N_GROUPS, GROUP, GROUP), lambda b, c: (l, 0, 0, 0)),
               vec(None),
               pl.BlockSpec((None, N_GROUPS, GROUP, GROUP), lambda b, c: (l, 0, 0, 0)),
               vec(None), vec(None)]
  args += [prm["conv_w"], prm["conv_b"], prm["wa_bd"], prm["lru_ba"], prm["wx_bd"], prm["lru_bx"],
           prm["lru_a_param"]]
  scratch = [pltpu.VMEM((8, D_MODEL), F32), pltpu.VMEM((8, D_MODEL), F32)] if chain else []
  return pl.pallas_call(
      functools.partial(_lru_kernel, chain),
      grid=grid,
      in_specs=in_specs,
      out_specs=[pl.BlockSpec((rows, D_MODEL), lambda b, c: (rmap(b, c)[0] - blk0, 0)), hl_spec],
      out_shape=[jax.ShapeDtypeStruct((nseq * tlen, D_MODEL), F32),
                 jax.ShapeDtypeStruct((nseq, 1, D_MODEL), F32)],
      scratch_shapes=scratch,
      compiler_params=pltpu.CompilerParams(
          dimension_semantics=("arbitrary", "arbitrary"), vmem_limit_bytes=VMEM_LIMIT),
      name="lru_chain" if chain else "lru_step",
  )(*args)


def _wkv_masks(tlen):
  lt = tlen.bit_length() - 1
  ri = _iota((CHUNK, GROUP), 0)
  cm = _iota((CHUNK, GROUP), 1) & (CHUNK - 1)
  same = (ri >> lt) == (cm >> lt)
  r2 = _iota((GROUP, GROUP), 0)
  c2 = _iota((GROUP, GROUP), 1)
  rs = _iota((2 * CHUNK, CHUNK), 0)
  cs = _iota((2 * CHUNK, CHUNK), 1)
  rr = rs & (CHUNK - 1)
  same_s = (rr >> lt) == (cs >> lt)
  cum_mask = jnp.logical_and(same_s, jnp.logical_or(cs <= rr, rs >= CHUNK))
  bd = (r2 >> 6) == (c2 >> 6)
  return dict(
      strict=jnp.logical_and(same, cm < ri),
      incl=jnp.logical_and(same, cm <= ri),
      eye=jnp.where(cm == ri, 1.0, 0.0),
      bd=bd,
      bd_ones=jnp.where(bd, 1.0, 0.0).astype(BF16),
      cum=jnp.where(cum_mask, 1.0, 0.0).astype(BF16),
      colseq=(_iota((GROUP, 2 * CHUNK), 1) & (CHUNK - 1)) >> lt,
      lt=lt,
  )


def _bd(x, mk):
  return jnp.where(mk["bd"], _tile4(x), 0.0)


def _headsum(x, mk):
  return _mm(x, mk["bd_ones"], na=2)


def _wkv_group(r, k, v, av, bv, w, cum, tot, states, mk, tlen, np_gram, np_inv, np_state):
  nseq = CHUNK // tlen
  e_in = jnp.exp(cum)
  e_ex = jnp.exp(cum - w)
  e_neg = jnp.exp(-cum)
  e_rem = jnp.exp(tot - cum)
  at = av * e_ex
  rt = r * e_in
  bt = bv * e_neg
  kt = k * e_neg
  bh = bv * e_rem
  kh = k * e_rem

  lhs = jnp.concatenate([at, rt], axis=0)
  gb = _mm(lhs, _bd(bt, mk), np_gram, np_gram, trans_b=True)
  gk = _mm(lhs, _bd(kt, mk), np_gram, np_gram, trans_b=True)
  n_ab = jnp.where(mk["strict"], gb[:CHUNK], 0.0)
  a_ak = jnp.where(mk["strict"], gk[:CHUNK], 0.0)
  a_rb = jnp.where(mk["incl"], gb[CHUNK:], 0.0)
  a_rk = jnp.where(mk["incl"], gk[CHUNK:], 0.0)

  t = mk["eye"] + n_ab
  q = _mm(n_ab, _bd(n_ab, mk), np_inv, np_inv)
  levels = mk["lt"]
  for lvl in range(1, levels):
    bq = _bd(q, mk)
    if lvl + 1 < levels:
      tq = _mm(jnp.concatenate([t, q], axis=0), bq, np_inv, np_inv)
      t = t + tq[:CHUNK]
      q = tq[CHUNK:]
    else:
      t = t + _mm(t, bq, np_inv, np_inv)

  bdv = _bd(v, mk)
  a_v = _mm(a_ak, bdv, np_state, np_state)

  if nseq == 1:
    lp = _mm(lhs, states[0], np_state, np_state, trans_b=True)
    a_p, r_p = lp[:CHUNK], lp[CHUNK:]
  else:
    aps, rps = [], []
    for j in range(nseq):
      sl = slice(j * tlen, (j + 1) * tlen)
      lj = jnp.concatenate([at[sl], rt[sl]], axis=0)
      lpj = _mm(lj, states[j], np_state, np_state, trans_b=True)
      aps.append(lpj[:tlen])
      rps.append(lpj[tlen:])
    a_p = jnp.concatenate(aps, axis=0)
    r_p = jnp.concatenate(rps, axis=0)
  u = _mm(t, _bd(a_p + a_v, mk), np_state, np_state)
  o = r_p + _mm(jnp.concatenate([a_rb, a_rk], axis=1), jnp.concatenate([_bd(u, mk), bdv], axis=0),
                np_state, np_state)

  x = jnp.concatenate([bh, kh], axis=0)
  yt = jnp.concatenate([u, v], axis=0).T
  new_states = []
  for j in range(nseq):
    ytj = yt if nseq == 1 else jnp.where(mk["colseq"] == j, yt, 0.0)
    d_c = jnp.exp(tot[j * tlen:j * tlen + 1, :])
    upd = _mm(ytj, x, np_state, np_state)
    new_states.append(jnp.where(mk["bd"], states[j] * d_c + upd, 0.0))
  return o, new_states


def _wkv_kernel(chain, tlen, *refs):
  if chain:
    (sh_ref, gb_ref, pa_ref, mu_ref, lw_ref, w0_ref, a0_ref, gup_ref, kk_ref, ka_ref, rk_ref,
     gnw_ref, gnb_ref, out_ref, sout_ref, s_scr, carry_scr) = refs
  else:
    (sh_ref, gb_ref, pa_ref, sp_ref, s0_ref, mu_ref, lw_ref, w0_ref, a0_ref, gup_ref, kk_ref, ka_ref,
     rk_ref, gnw_ref, gnb_ref, out_ref, sout_ref) = refs
  nseq = CHUNK // tlen
  mk = _wkv_masks(tlen)

  sh = sh_ref[...]
  rolled = pltpu.roll(sh, 1, 0)
  row = _iota((CHUNK, SHIFT_WIDTH), 0)
  if chain:
    c = pl.program_id(1)

    @pl.when(c == 0)
    def _():
      carry_scr[...] = jnp.zeros_like(carry_scr)
      s_scr[...] = jnp.zeros_like(s_scr)

    prev = jnp.where(row == 0, carry_scr[7:8, :], rolled)
    carry_scr[...] = sh_ref[CHUNK - 8:CHUNK, :]
  else:
    prev = jnp.where((row & (tlen - 1)) == 0, sp_ref[...], rolled)
  mixed = sh + (prev - sh) * mu_ref[...]
  r = mixed[:, 0:D_MODEL]
  k = mixed[:, D_MODEL:2 * D_MODEL]
  v = mixed[:, 2 * D_MODEL:3 * D_MODEL]
  xl = mixed[:, 3 * D_MODEL:3 * D_MODEL + 128]
  xg = mixed[:, 3 * D_MODEL + 128:SHIFT_WIDTH]
  lane = _iota((CHUNK, 128), 1)
  lin = jnp.where(lane < 64, jnp.tanh(xl), xl).astype(BF16)
  lo = _dot(lin, lw_ref[...])
  w_log = -_softplus(-(w0_ref[...] + lo[:, :D_MODEL])) - 0.5
  w = -jnp.exp(w_log)
  a = jax.nn.sigmoid(a0_ref[...] + lo[:, D_MODEL:])
  g = _dot(jax.nn.sigmoid(xg).astype(BF16), gup_ref[...])
  kk = k * kk_ref[...]
  k_h = k * (1.0 + (a - 1.0) * ka_ref[...])
  rkr = r * k_h * rk_ref[...]

  cw = _mm(mk["cum"], w, nb=3)
  cum_all, tot_all = cw[:CHUNK], cw[CHUNK:]

  outs = []
  for gi in range(N_GROUPS):
    sl = slice(gi * GROUP, (gi + 1) * GROUP)
    kk_g = kk[:, sl]
    ss = _headsum(kk_g * kk_g, mk)
    kk_g = kk_g / jnp.maximum(jnp.sqrt(ss), 1e-12)
    a_g = a[:, sl]
    if chain:
      states = [s_scr[gi]]
    else:
      states = [jnp.where(mk["bd"], _tile4(s0_ref[j, gi]), 0.0) for j in range(nseq)]
    o, new_states = _wkv_group(r[:, sl], k_h[:, sl], v[:, sl], -kk_g, kk_g * a_g, w[:, sl],
                               cum_all[:, sl], tot_all[:, sl], states, mk, tlen, 2, 2, 2)
    if chain:
      s_scr[gi] = new_states[0]
    for j in range(nseq):
      sn = new_states[j]
      sout_ref[j, gi] = sn[0:64] + sn[64:128] + sn[128:192] + sn[192:256]
    o_mu = _headsum(o, mk) * (1.0 / HEAD)
    oc = o - o_mu
    o_var = _headsum(oc * oc, mk) * (1.0 / HEAD)
    o_n = oc * lax.rsqrt(o_var + GN_EPS) * gnw_ref[:, sl] + gnb_ref[:, sl]
    bonus = _headsum(rkr[:, sl], mk) * v[:, sl]
    outs.append((o_n + bonus) * g[:, sl])
  out_b = jnp.concatenate(outs, axis=1)
  out_ref[...] = pa_ref[...] + jax.nn.sigmoid(gb_ref[...]) * out_b


def _wkv(sh, gates, pa, prm, l, *, chain, nseq, tlen, row0, sprev=None, s0=None):
  blk0 = row0 // CHUNK
  if chain:
    nchunk = tlen // CHUNK
    grid = (nseq, nchunk)
    rmap = lambda b, c: (b * nchunk + c, 0)
    smap = lambda b, c: (b, 0, 0, 0)
    sblk = 1
  else:
    sblk = CHUNK // tlen
    grid = (nseq // sblk, 1)
    rmap = lambda b, c: (b, 0)
    smap = lambda b, c: (b, 0, 0, 0)
  gmap = lambda b, c: (blk0 + rmap(b, c)[0], 0)
  gmap1 = lambda b, c: (blk0 + rmap(b, c)[0], 1)
  vec = pl.BlockSpec((None, 1, D_MODEL), lambda b, c: (l, 0, 0))
  in_specs = [pl.BlockSpec((CHUNK, SHIFT_WIDTH), gmap),
              pl.BlockSpec((CHUNK, D_MODEL), gmap1),
              pl.BlockSpec((CHUNK, D_MODEL), rmap)]
  args = [sh, gates, pa]
  if not chain:
    in_specs += [pl.BlockSpec((CHUNK, SHIFT_WIDTH), rmap),
                 pl.BlockSpec((sblk, N_GROUPS, HEAD, GROUP), smap)]
    args += [sprev, s0]
  in_specs += [pl.BlockSpec((None, 1, SHIFT_WIDTH), lambda b, c: (l, 0, 0)),
               pl.BlockSpec((None, 128, 2 * D_MODEL), lambda b, c: (l, 0, 0)),
               vec, vec,
               pl.BlockSpec((None, 128, D_MODEL), lambda b, c: (l, 0, 0)),
               vec, vec, vec, vec, vec]
  args += [prm["shift_mu"], prm["lora_w"], prm["w0"], prm["a0"], prm["gate_up"], prm["k_k"], prm["k_a"],
           prm["r_k"], prm["gn_w"], prm["gn_b"]]
  scratch = ([pltpu.VMEM((N_GROUPS, GROUP, GROUP), F32), pltpu.VMEM((8, SHIFT_WIDTH), F32)]
             if chain else [])
  return pl.pallas_call(
      functools.partial(_wkv_kernel, chain, CHUNK if chain else tlen),
      grid=grid,
      in_specs=in_specs,
      out_specs=[pl.BlockSpec((CHUNK, D_MODEL), rmap),
                 pl.BlockSpec((sblk, N_GROUPS, HEAD, GROUP), smap)],
      out_shape=[jax.ShapeDtypeStruct((nseq * tlen, D_MODEL), F32),
                 jax.ShapeDtypeStruct((nseq, N_GROUPS, HEAD, GROUP), F32)],
      scratch_shapes=scratch,
      compiler_params=pltpu.CompilerParams(
          dimension_semantics=("arbitrary", "arbitrary"), vmem_limit_bytes=VMEM_LIMIT),
      name="wkv_chain" if chain else "wkv_step",
  )(*args)


def _mlp_kernel(n_p_tiles, x_ref, mp_ref, ms_ref, wo_ref, g1_ref, b1_ref, w1_ref, w2_ref, g2_ref, b2_ref,
                o_ref, x1_scr, x1b_scr, acc_scr):
  i = pl.program_id(0)
  j = pl.program_id(1)

  @pl.when(j == 0)
  def _():
    merged = jnp.where(i < n_p_tiles, mp_ref[...], ms_ref[...])
    y = DN_ALPHA * x_ref[...] + _dot(merged.astype(BF16), wo_ref[...])
    x1 = _layer_norm(y, g1_ref[...], b1_ref[...])
    x1_scr[...] = x1
    x1b_scr[...] = x1.astype(BF16)
    acc_scr[...] = jnp.zeros_like(acc_scr)

  hid = jnp.square(jnp.maximum(_dot(x1b_scr[...], w1_ref[...]), 0.0))
  acc_scr[...] += _dot(hid.astype(BF16), w2_ref[...])

  @pl.when(j == pl.num_programs(1) - 1)
  def _():
    o_ref[...] = _layer_norm(DN_ALPHA * x1_scr[...] + acc_scr[...], g2_ref[...], b2_ref[...])


def _mlp(x, merged_p, merged_s, prm, l, tm=512, tf=1024):
  n = x.shape[0]
  n_p = merged_p.shape[0] // tm
  n_s = merged_s.shape[0] // tm
  vec = pl.BlockSpec((None, 1, D_MODEL), lambda i, j: (l, 0, 0))
  return pl.pallas_call(
      functools.partial(_mlp_kernel, n_p),
      grid=(n // tm, D_FF // tf),
      in_specs=[
          pl.BlockSpec((tm, D_MODEL), lambda i, j: (i, 0)),
          pl.BlockSpec((tm, D_MODEL), lambda i, j: (jnp.minimum(i, n_p - 1), 0)),
          pl.BlockSpec((tm, D_MODEL), lambda i, j: (jnp.clip(i - n_p, 0, n_s - 1), 0)),
          pl.BlockSpec((None, D_MODEL, D_MODEL), lambda i, j: (l, 0, 0)),
          vec, vec,
          pl.BlockSpec((None, D_MODEL, tf), lambda i, j: (l, 0, j)),
          pl.BlockSpec((None, tf, D_MODEL), lambda i, j: (l, j, 0)),
          vec, vec,
      ],
      out_specs=pl.BlockSpec((tm, D_MODEL), lambda i, j: (i, 0)),
      out_shape=jax.ShapeDtypeStruct((n, D_MODEL), F32),
      scratch_shapes=[pltpu.VMEM((tm, D_MODEL), F32), pltpu.VMEM((tm, D_MODEL), BF16),
                      pltpu.VMEM((tm, D_MODEL), F32)],
      compiler_params=pltpu.CompilerParams(
          dimension_semantics=("arbitrary", "arbitrary"), vmem_limit_bytes=VMEM_LIMIT),
      name="outproj_mlp",
  )(x, merged_p, merged_s, prm["w_out"], prm["ln1_g"], prm["ln1_b"], prm["mlp_w1"], prm["mlp_w2"],
    prm["ln2_g"], prm["ln2_b"])


def _block_diag4(w):
  depth = w.shape[0]
  w5 = w.reshape(depth, N_GROUPS, 4, HEAD, HEAD)
  eye = jnp.eye(4, dtype=w.dtype)
  return jnp.einsum("lgaij,ab->lgaibj", w5, eye).reshape(depth, N_GROUPS, GROUP, GROUP)


def _state_to_cat(s):
  b = s.shape[0]
  return s.reshape(b, N_GROUPS, 4, HEAD, HEAD).transpose(0, 1, 3, 2, 4).reshape(b, N_GROUPS, HEAD, GROUP)


def _cat_to_state(z):
  b = z.shape[0]
  return z.reshape(b, N_GROUPS, HEAD, 4, HEAD).transpose(0, 1, 3, 2, 4).reshape(b, 4 * N_GROUPS, HEAD, HEAD)


def kernel(x_prompt, x_sample, state_conv, state_lru, state_shift, state_wkv, w_in, conv_w, conv_b, lru_wa,
           lru_ba, lru_wx, lru_bx, lru_a_param, shift_mu, decay_up, w0, aaa_up, a0, gate_up, k_k, k_a, r_k,
           gn_w, gn_b, w_out, ln1_g, ln1_b, mlp_w1, mlp_w2, ln2_g, ln2_b):
  bp, tp, _ = x_prompt.shape
  bs, ts, _ = x_sample.shape
  depth = w_in.shape[0]
  n_p = bp * tp
  n_s = bs * ts

  row = lambda p: p.reshape(depth, 1, -1)
  w_in_b = w_in.astype(BF16)
  zero = jnp.zeros((depth, 64, D_MODEL), F32)
  lora_w = jnp.concatenate([jnp.concatenate([decay_up, zero], axis=2),
                            jnp.concatenate([zero, aaa_up], axis=2)], axis=1).astype(BF16)
  prm = dict(
      conv_w=conv_w, conv_b=row(conv_b),
      wa_bd=_block_diag4(lru_wa).astype(BF16), wx_bd=_block_diag4(lru_wx).astype(BF16),
      lru_ba=row(lru_ba), lru_bx=row(lru_bx), lru_a_param=row(lru_a_param),
      shift_mu=row(shift_mu), lora_w=lora_w, w0=row(w0), a0=row(a0), gate_up=gate_up.astype(BF16),
      k_k=row(k_k), k_a=row(k_a), r_k=row(r_k), gn_w=row(gn_w), gn_b=row(gn_b),
      w_out=w_out.astype(BF16), ln1_g=row(ln1_g), ln1_b=row(ln1_b),
      mlp_w1=mlp_w1.astype(BF16), mlp_w2=mlp_w2.astype(BF16), ln2_g=row(ln2_g), ln2_b=row(ln2_b),
  )
  w_xy = w_in_b[:, :, :2 * D_MODEL]
  w_sh = w_in_b[:, :, 2 * D_MODEL:2 * D_MODEL + SHIFT_WIDTH]
  w_gt = w_in_b[:, :, 2 * D_MODEL + SHIFT_WIDTH:]

  x = jnp.concatenate([x_prompt.reshape(n_p, D_MODEL), x_sample.reshape(n_s, D_MODEL)], axis=0)
  conv_p, lru_p, shift_p, wkv_p = [], [], [], []
  conv_s, lru_s, shift_s, wkv_s = [], [], [], []
  for l in range(depth):
    xy = _inproj(x, w_xy, l)
    sh = _inproj(x, w_sh, l)
    gt = _inproj(x, w_gt, l)

    pa_p, hl_p = _lru(xy, gt, prm, l, chain=True, nseq=bp, tlen=tp, row0=0)
    prev_s = jnp.pad(state_conv[l], ((0, 0), (5, 0), (0, 0))).reshape(bs * 8, D_MODEL)
    pa_s, hl_s = _lru(xy, gt, prm, l, chain=False, nseq=bs, tlen=ts, row0=n_p,
                      prev=prev_s, h0=state_lru[l].reshape(bs, 1, D_MODEL))

    mg_p, z_p = _wkv(sh, gt, pa_p, prm, l, chain=True, nseq=bp, tlen=tp, row0=0)
    sprev = jnp.repeat(state_shift[l], ts, axis=0)
    mg_s, z_s = _wkv(sh, gt, pa_s, prm, l, chain=False, nseq=bs, tlen=ts, row0=n_p,
                     sprev=sprev, s0=_state_to_cat(state_wkv[l]))

    x = _mlp(x, mg_p, mg_s, prm, l)

    lx_p = xy[:n_p].reshape(bp, tp, 2 * D_MODEL)
    lx_s = xy[n_p:].reshape(bs, ts, 2 * D_MODEL)
    conv_p.append(lx_p[:, tp - 3:, :D_MODEL])
    conv_s.append(lx_s[:, ts - 3:, :D_MODEL])
    lru_p.append(hl_p.reshape(bp, D_MODEL))
    lru_s.append(hl_s.reshape(bs, D_MODEL))
    shift_p.append(sh[:n_p].reshape(bp, tp, SHIFT_WIDTH)[:, tp - 1])
    shift_s.append(sh[n_p:].reshape(bs, ts, SHIFT_WIDTH)[:, ts - 1])
    wkv_p.append(_cat_to_state(z_p))
    wkv_s.append(_cat_to_state(z_s))

  return (x[:n_p].reshape(bp, tp, D_MODEL), x[n_p:].reshape(bs, ts, D_MODEL),
          jnp.stack(conv_p), jnp.stack(lru_p), jnp.stack(shift_p), jnp.stack(wkv_p),
          jnp.stack(conv_s), jnp.stack(lru_s), jnp.stack(shift_s), jnp.stack(wkv_s))
```

```python
import functools
import math

import jax
import jax.numpy as jnp
from jax import lax
from jax.experimental import pallas as pl
from jax.experimental.pallas import tpu as pltpu

F32 = jnp.float32
BF16 = jnp.bfloat16

D_MODEL = 1024
DEPTH = 4
LRU_C = 8.0
HEAD = 64
GROUP = 256
N_GROUPS = D_MODEL // GROUP
SHIFT_WIDTH = 3 * D_MODEL + 64 + 64 + 128
D_FF = 4 * D_MODEL
DN_ALPHA = (2 * DEPTH) ** 0.25
LN_EPS = 1e-5
GN_EPS = 64e-5
CHUNK = 64
VMEM_LIMIT = 56 * 1024 * 1024
NP_GRAM, NP_INV, NP_STATE = 1, 1, 1


def _split(x, n):
  pieces = []
  rem = x
  for i in range(n):
    p = rem.astype(BF16)
    pieces.append(p)
    if i + 1 < n:
      rem = rem - p.astype(F32)
  return pieces


def _dot(a, b, trans_b=False):
  dims = (((1,), (1,)), ((), ())) if trans_b else (((1,), (0,)), ((), ()))
  return lax.dot_general(a, b, dims, preferred_element_type=F32)


def _mm(a, b, na=1, nb=1, trans_b=False):
  ap = a if isinstance(a, list) else ([a] if a.dtype == BF16 else _split(a, na))
  bp = b if isinstance(b, list) else ([b] if b.dtype == BF16 else _split(b, nb))
  order = max(len(ap), len(bp))
  acc = None
  for i, x in enumerate(ap):
    for j, y in enumerate(bp):
      if i + j < order:
        t = _dot(x, y, trans_b)
        acc = t if acc is None else acc + t
  return acc


def _softplus(x):
  return jnp.maximum(x, 0.0) + jnp.log1p(jnp.exp(-jnp.abs(x)))


def _expm1(x):
  u = jnp.exp(x)
  lu = jnp.log(u)
  small = jnp.where(u == 1.0, x, (u - 1.0) * x / jnp.where(u == 1.0, 1.0, lu))
  return jnp.where(jnp.abs(x) < 0.5, small, u - 1.0)


def _gelu_tanh(x):
  c = math.sqrt(2.0 / math.pi)
  return x * (0.5 * (1.0 + jnp.tanh(c * (x + 0.044715 * (x * x * x)))))


def _layer_norm(x, g, b):
  mu = jnp.mean(x, axis=-1, keepdims=True)
  xc = x - mu
  var = jnp.mean(xc * xc, axis=-1, keepdims=True)
  return xc * lax.rsqrt(var + LN_EPS) * g + b


def _iota(shape, dim):
  return lax.broadcasted_iota(jnp.int32, shape, dim)


def _tile4(x):
  return jnp.concatenate([x, x, x, x], axis=0)


def _cat(xs, axis):
  return xs[0] if len(xs) == 1 else jnp.concatenate(xs, axis=axis)


def _inproj_kernel(x_ref, w_ref, o_ref):
  o_ref[...] = _dot(x_ref[...].astype(BF16), w_ref[...])


def _inproj(x, w, l, tm=512):
  n = x.shape[0]
  width = w.shape[2]
  return pl.pallas_call(
      _inproj_kernel,
      grid=(n // tm,),
      in_specs=[
          pl.BlockSpec((tm, D_MODEL), lambda i: (i, 0)),
          pl.BlockSpec((None, D_MODEL, width), lambda i: (l, 0, 0)),
      ],
      out_specs=pl.BlockSpec((tm, width), lambda i: (i, 0)),
      out_shape=jax.ShapeDtypeStruct((n, width), F32),
      compiler_params=pltpu.CompilerParams(
          dimension_semantics=("arbitrary",), vmem_limit_bytes=VMEM_LIMIT),
      name="inproj",
  )(x, w)


def _lru_kernel(chain, *refs):
  if chain:
    (x_ref, y_ref, ga_ref, cw_ref, cb_ref, wa_ref, ba_ref, wx_ref, bx_ref, ap_ref,
     pa_ref, hl_ref, cx_scr, ch_scr) = refs
  else:
    (x_ref, y_ref, ga_ref, prev_ref, h0_ref, cw_ref, cb_ref, wa_ref, ba_ref, wx_ref, bx_ref,
     ap_ref, pa_ref, hl_ref) = refs
  rows = x_ref.shape[0]
  nt = rows // 8
  x = x_ref[...]
  x3 = x.reshape(nt, 8, D_MODEL)
  if chain:
    c = pl.program_id(1)

    @pl.when(c == 0)
    def _():
      cx_scr[...] = jnp.zeros_like(cx_scr)
      ch_scr[...] = jnp.zeros_like(ch_scr)

    prev3 = jnp.concatenate([cx_scr[...].reshape(1, 8, D_MODEL), x3[:nt - 1]], axis=0)
    cx_scr[...] = x_ref[rows - 8:rows, :]
  else:
    prev3 = prev_ref[...].reshape(nt, 8, D_MODEL)

  sub = _iota((nt, 8, D_MODEL), 1)
  u = cb_ref[...] + cw_ref[3:4, :] * x
  for j in (1, 2, 3):
    xs = jnp.where(sub >= j, pltpu.roll(x3, j, 1), pltpu.roll(prev3, j, 1))
    u = u + cw_ref[3 - j:4 - j, :] * xs.reshape(rows, D_MODEL)

  gr, gi = [], []
  for g in range(N_GROUPS):
    ug = u[:, g * GROUP:(g + 1) * GROUP].astype(BF16)
    gr.append(_dot(ug, wa_ref[g]))
    gi.append(_dot(ug, wx_ref[g]))
  gate_r = jax.nn.sigmoid(jnp.concatenate(gr, axis=1) + ba_ref[...])
  gate_i = jax.nn.sigmoid(jnp.concatenate(gi, axis=1) + bx_ref[...])
  log_a = (-LRU_C * gate_r) * _softplus(ap_ref[...])
  a = jnp.exp(log_a)
  mult = jnp.sqrt(-_expm1(2.0 * log_a))
  if chain:
    first = jnp.logical_and(_iota((rows, D_MODEL), 0) == 0, c == 0)
    mult = jnp.where(first, 1.0, mult)
  xin = u * gate_i * mult

  a3 = a.reshape(nt, 8, D_MODEL)
  b3 = xin.reshape(nt, 8, D_MODEL)
  for s in (1, 2, 4):
    m = sub >= s
    b3 = jnp.where(m, a3 * pltpu.roll(b3, s, 1) + b3, b3)
    a3 = jnp.where(m, a3 * pltpu.roll(a3, s, 1), a3)
  if chain:
    hc = ch_scr[7:8, :]
    hs = []
    for i in range(nt):
      hi = a3[i] * hc + b3[i]
      hs.append(hi)
      hc = hi[7:8, :]
    h = jnp.concatenate(hs, axis=0)
    ch_scr[...] = hs[-1]
    hl_ref[0] = hc
  else:
    h3 = a3 * h0_ref[...] + b3
    hl_ref[...] = h3[:, 7:8, :]
    h = h3.reshape(rows, D_MODEL)

  pa_ref[...] = jax.nn.sigmoid(ga_ref[...]) * (h * _gelu_tanh(y_ref[...]))


def _lru(xy, gates, prm, l, *, chain, nseq, tlen, row0, prev=None, h0=None):
  if chain:
    rows = 256
    grid = (nseq, tlen // rows)
    blk0 = row0 // rows
    rmap = lambda b, c: (blk0 + b * (tlen // rows) + c, 0)
    hl_spec = pl.BlockSpec((1, 1, D_MODEL), lambda b, c: (b, 0, 0))
  else:
    nb = 16
    rows = nb * tlen
    grid = (nseq // nb, 1)
    blk0 = row0 // rows
    rmap = lambda b, c: (blk0 + b, 0)
    hl_spec = pl.BlockSpec((nb, 1, D_MODEL), lambda b, c: (b, 0, 0))
  rmap1 = lambda b, c: (rmap(b, c)[0], 1)
  vec = lambda arr: pl.BlockSpec((None, 1, D_MODEL), lambda b, c: (l, 0, 0))
  in_specs = [pl.BlockSpec((rows, D_MODEL), rmap),
              pl.BlockSpec((rows, D_MODEL), rmap1),
              pl.BlockSpec((rows, D_MODEL), rmap)]
  args = [xy, xy, gates]
  if not chain:
    in_specs += [pl.BlockSpec((rows, D_MODEL), lambda b, c: (b, 0)),
                 pl.BlockSpec((nb, 1, D_MODEL), lambda b, c: (b, 0, 0))]
    args += [prev, h0]
  in_specs += [pl.BlockSpec((None, 4, D_MODEL), lambda b, c: (l, 0, 0)),
               vec(None),
               pl.BlockSpec((None, N_GROUPS, GROUP, GROUP), lambda b, c: (l, 0, 0, 0)),
               vec(None),
               pl.BlockSpec((None, N_GROUPS, GROUP, GROUP), lambda b, c: (l, 0, 0, 0)),
               vec(None), vec(None)]
  args += [prm["conv_w"], prm["conv_b"], prm["wa_bd"], prm["lru_ba"], prm["wx_bd"], prm["lru_bx"],
           prm["lru_a_param"]]
  scratch = [pltpu.VMEM((8, D_MODEL), F32), pltpu.VMEM((8, D_MODEL), F32)] if chain else []
  return pl.pallas_call(
      functools.partial(_lru_kernel, chain),
      grid=grid,
      in_specs=in_specs,
      out_specs=[pl.BlockSpec((rows, D_MODEL), lambda b, c: (rmap(b, c)[0] - blk0, 0)), hl_spec],
      out_shape=[jax.ShapeDtypeStruct((nseq * tlen, D_MODEL), F32),
                 jax.ShapeDtypeStruct((nseq, 1, D_MODEL), F32)],
      scratch_shapes=scratch,
      compiler_params=pltpu.CompilerParams(
          dimension_semantics=("arbitrary", "arbitrary"), vmem_limit_bytes=VMEM_LIMIT),
      name="lru_chain" if chain else "lru_step",
  )(*args)


def _wkv_masks(tlen):
  lt = tlen.bit_length() - 1
  ri = _iota((CHUNK, GROUP), 0)
  cm = _iota((CHUNK, GROUP), 1) & (CHUNK - 1)
  same = (ri >> lt) == (cm >> lt)
  r2 = _iota((GROUP, GROUP), 0)
  c2 = _iota((GROUP, GROUP), 1)
  rs = _iota((2 * CHUNK, CHUNK), 0)
  cs = _iota((2 * CHUNK, CHUNK), 1)
  rr = rs & (CHUNK - 1)
  same_s = (rr >> lt) == (cs >> lt)
  cum_mask = jnp.logical_and(same_s, jnp.logical_or(cs <= rr, rs >= CHUNK))
  bd = (r2 >> 6) == (c2 >> 6)
  return dict(
      strict=jnp.logical_and(same, cm < ri),
      incl=jnp.logical_and(same, cm <= ri),
      eye=jnp.where(cm == ri, 1.0, 0.0),
      bd=bd,
      bd_ones=jnp.where(bd, 1.0, 0.0).astype(BF16),
      cum=jnp.where(cum_mask, 1.0, 0.0).astype(BF16),
      colseq=(_iota((GROUP, 2 * CHUNK), 1) & (CHUNK - 1)) >> lt,
      lt=lt,
  )


def _bd(x, mk, n=1):
  return [_tile4(p) * mk["bd_ones"] for p in _split(x, n)]


def _headsum(x, mk):
  return _mm(x, mk["bd_ones"], na=2)


def _rowblock_sum(p):
  return p[0:64] + p[64:128] + p[128:192] + p[192:256]


def _wkv_kernel(chain, tlen, cps, *refs):
  if chain:
    (sh_ref, gb_ref, pa_ref, mu_ref, lw_ref, w0_ref, a0_ref, gup_ref, kk_ref, ka_ref, rk_ref,
     gnw_ref, gnb_ref, out_ref, sout_ref, s_scr, carry_scr) = refs
  else:
    (sh_ref, gb_ref, pa_ref, sp_ref, s0_ref, mu_ref, lw_ref, w0_ref, a0_ref, gup_ref, kk_ref, ka_ref,
     rk_ref, gnw_ref, gnb_ref, out_ref, sout_ref) = refs
  rows = cps * CHUNK
  nsq = CHUNK // tlen
  mk = _wkv_masks(tlen)
  groups = range(N_GROUPS)
  units = [(ci, gi) for ci in range(cps) for gi in groups]
  gsl = lambda gi: slice(gi * GROUP, (gi + 1) * GROUP)
  cut = lambda x, un: x[un[0] * CHUNK:(un[0] + 1) * CHUNK, gsl(un[1])]

  sh = sh_ref[...]
  rolled = pltpu.roll(sh, 1, 0)
  row = _iota((rows, SHIFT_WIDTH), 0)
  if chain:
    c = pl.program_id(1)

    @pl.when(c == 0)
    def _():
      carry_scr[...] = jnp.zeros_like(carry_scr)
      s_scr[...] = jnp.zeros_like(s_scr)

    prev = jnp.where(row == 0, carry_scr[7:8, :], rolled)
    carry_scr[...] = sh_ref[rows - 8:rows, :]
  else:
    prev = jnp.where((row & (tlen - 1)) == 0, sp_ref[...], rolled)
  mixed = sh + (prev - sh) * mu_ref[...]
  r = mixed[:, 0:D_MODEL]
  k = mixed[:, D_MODEL:2 * D_MODEL]
  v = mixed[:, 2 * D_MODEL:3 * D_MODEL]
  xl = mixed[:, 3 * D_MODEL:3 * D_MODEL + 128]
  xg = mixed[:, 3 * D_MODEL + 128:SHIFT_WIDTH]
  lane = _iota((rows, 128), 1)
  lin = jnp.where(lane < 64, jnp.tanh(xl), xl).astype(BF16)
  lo = _dot(lin, lw_ref[...])
  w_log = -_softplus(-(w0_ref[...] + lo[:, :D_MODEL])) - 0.5
  w = -jnp.exp(w_log)
  a = jax.nn.sigmoid(a0_ref[...] + lo[:, D_MODEL:])
  g = _dot(jax.nn.sigmoid(xg).astype(BF16), gup_ref[...])
  kk = k * kk_ref[...]
  k_h = k * (1.0 + (a - 1.0) * ka_ref[...])
  rkr = r * k_h * rk_ref[...]
  kkn = _cat([kk[:, gsl(gi)] / jnp.maximum(jnp.sqrt(_headsum(kk[:, gsl(gi)] * kk[:, gsl(gi)], mk)), 1e-12)
              for gi in groups], 1)
  av = -kkn
  bv = kkn * a

  cums, tots = [], []
  for ci in range(cps):
    cw = _mm(mk["cum"], w[ci * CHUNK:(ci + 1) * CHUNK], nb=3)
    cums.append(cw[:CHUNK])
    tots.append(cw[CHUNK:])
  cum = _cat(cums, 0)
  tot = _cat(tots, 0)
  at = av * jnp.exp(cum - w)
  rt = r * jnp.exp(cum)
  e_neg = jnp.exp(-cum)
  bt = bv * e_neg
  kt = k_h * e_neg
  e_rem = jnp.exp(tot - cum)
  bh = bv * e_rem
  kh = k_h * e_rem
  d_tot = jnp.exp(tot)

  lhs, n_ab, a_ak, a_rbk = {}, {}, {}, {}
  for un in units:
    lhs[un] = jnp.concatenate([cut(at, un), cut(rt, un)], axis=0)
    gb = _mm(lhs[un], _bd(cut(bt, un), mk, NP_GRAM), NP_GRAM, trans_b=True)
    gk = _mm(lhs[un], _bd(cut(kt, un), mk, NP_GRAM), NP_GRAM, trans_b=True)
    n_ab[un] = jnp.where(mk["strict"], gb[:CHUNK], 0.0)
    a_ak[un] = jnp.where(mk["strict"], gk[:CHUNK], 0.0)
    a_rbk[un] = jnp.concatenate([jnp.where(mk["incl"], gb[CHUNK:], 0.0),
                                 jnp.where(mk["incl"], gk[CHUNK:], 0.0)], axis=1)

  t = {un: mk["eye"] + n_ab[un] for un in units}
  q = {un: _mm(n_ab[un], _bd(n_ab[un], mk, NP_INV), NP_INV) for un in units}
  levels = mk["lt"]
  for lvl in range(1, levels):
    for un in units:
      bq = _bd(q[un], mk, NP_INV)
      if lvl + 1 < levels:
        tq = _mm(jnp.concatenate([t[un], q[un]], axis=0), bq, NP_INV)
        t[un] = t[un] + tq[:CHUNK]
        q[un] = tq[CHUNK:]
      else:
        t[un] = t[un] + _mm(t[un], bq, NP_INV)

  bdv, a_v, xt, dcol = {}, {}, {}, {}
  for un in units:
    bdv[un] = _bd(cut(v, un), mk, NP_STATE)
    a_v[un] = _mm(a_ak[un], bdv[un], NP_STATE)
    dt = cut(d_tot, un)
    xpt = jnp.concatenate([cut(bh, un), cut(kh, un), dt, dt], axis=0).T
    xt[un] = xpt[:, :2 * CHUNK]
    dcol[un] = xpt[:, 2 * CHUNK:]

  p_cur = {gi: s_scr[gi] for gi in groups} if chain else None
  o = {}
  for ci in range(cps):
    for gi in groups:
      un = (ci, gi)
      if chain:
        states = [p_cur[gi]]
      else:
        states = [jnp.where(mk["bd"], _tile4(s0_ref[ci * nsq + j, gi]), 0.0) for j in range(nsq)]
      if nsq == 1:
        lp = _mm(lhs[un], states[0], NP_STATE, NP_STATE)
        a_p, r_p = lp[:CHUNK], lp[CHUNK:]
      else:
        aps, rps = [], []
        for j in range(nsq):
          sl = slice(j * tlen, (j + 1) * tlen)
          lpj = _mm(jnp.concatenate([lhs[un][sl], lhs[un][CHUNK + j * tlen:CHUNK + (j + 1) * tlen]], axis=0),
                    states[j], NP_STATE, NP_STATE)
          aps.append(lpj[:tlen])
          rps.append(lpj[tlen:])
        a_p = jnp.concatenate(aps, axis=0)
        r_p = jnp.concatenate(rps, axis=0)
      u = _mm(t[un], _bd(a_p + a_v[un], mk, NP_STATE), NP_STATE)
      bdu = _bd(u, mk, NP_STATE)
      o[un] = r_p + _mm(a_rbk[un], [jnp.concatenate([pu, pv], axis=0) for pu, pv in zip(bdu, bdv[un])],
                        NP_STATE)
      y = jnp.concatenate([u, cut(v, un)], axis=0)
      for j in range(nsq):
        if nsq == 1:
          xtj = xt[un]
          dcb = jnp.concatenate([dcol[un], dcol[un]], axis=1)
        else:
          xtj = jnp.where(mk["colseq"] == j, xt[un], 0.0)
          dcb = jnp.broadcast_to(dcol[un][:, j * tlen:j * tlen + 1], (GROUP, GROUP))
        p_new = jnp.where(mk["bd"], states[j] * dcb + _mm(xtj, y, NP_STATE, NP_STATE), 0.0)
        if chain:
          p_cur[gi] = p_new
        else:
          sout_ref[ci * nsq + j, gi] = _rowblock_sum(p_new)
  if chain:
    for gi in groups:
      s_scr[gi] = p_cur[gi]

    @pl.when(c == pl.num_programs(1) - 1)
    def _():
      for gi in groups:
        sout_ref[0, gi] = _rowblock_sum(s_scr[gi])

  outs = []
  for gi in groups:
    og = _cat([o[(ci, gi)] for ci in range(cps)], 0)
    o_mu = _headsum(og, mk) * (1.0 / HEAD)
    oc = og - o_mu
    o_var = _headsum(oc * oc, mk) * (1.0 / HEAD)
    o_n = oc * lax.rsqrt(o_var + GN_EPS) * gnw_ref[:, gsl(gi)] + gnb_ref[:, gsl(gi)]
    bonus = _headsum(rkr[:, gsl(gi)], mk) * v[:, gsl(gi)]
    outs.append((o_n + bonus) * g[:, gsl(gi)])
  out_ref[...] = pa_ref[...] + jax.nn.sigmoid(gb_ref[...]) * jnp.concatenate(outs, axis=1)


def _wkv(sh, gates, pa, prm, l, *, chain, nseq, tlen, row0, cps, sprev=None, s0=None):
  rows = cps * CHUNK
  blk0 = row0 // rows
  if chain:
    nstep = tlen // rows
    grid = (nseq, nstep)
    rmap = lambda b, c: (b * nstep + c, 0)
    sblk = 1
    ktlen = CHUNK
  else:
    sblk = rows // tlen
    grid = (nseq // sblk, 1)
    rmap = lambda b, c: (b, 0)
    ktlen = tlen
  smap = lambda b, c: (b, 0, 0, 0)
  gmap = lambda b, c: (blk0 + rmap(b, c)[0], 0)
  gmap1 = lambda b, c: (blk0 + rmap(b, c)[0], 1)
  vec = pl.BlockSpec((None, 1, D_MODEL), lambda b, c: (l, 0, 0))
  in_specs = [pl.BlockSpec((rows, SHIFT_WIDTH), gmap),
              pl.BlockSpec((rows, D_MODEL), gmap1),
              pl.BlockSpec((rows, D_MODEL), rmap)]
  args = [sh, gates, pa]
  if not chain:
    in_specs += [pl.BlockSpec((rows, SHIFT_WIDTH), rmap),
                 pl.BlockSpec((sblk, N_GROUPS, HEAD, GROUP), smap)]
    args += [sprev, s0]
  in_specs += [pl.BlockSpec((None, 1, SHIFT_WIDTH), lambda b, c: (l, 0, 0)),
               pl.BlockSpec((None, 128, 2 * D_MODEL), lambda b, c: (l, 0, 0)),
               vec, vec,
               pl.BlockSpec((None, 128, D_MODEL), lambda b, c: (l, 0, 0)),
               vec, vec, vec, vec, vec]
  args += [prm["shift_mu"], prm["lora_w"], prm["w0"], prm["a0"], prm["gate_up"], prm["k_k"], prm["k_a"],
           prm["r_k"], prm["gn_w"], prm["gn_b"]]
  scratch = ([pltpu.VMEM((N_GROUPS, GROUP, GROUP), F32), pltpu.VMEM((8, SHIFT_WIDTH), F32)]
             if chain else [])
  return pl.pallas_call(
      functools.partial(_wkv_kernel, chain, ktlen, cps),
      grid=grid,
      in_specs=in_specs,
      out_specs=[pl.BlockSpec((rows, D_MODEL), rmap),
                 pl.BlockSpec((sblk, N_GROUPS, HEAD, GROUP), smap)],
      out_shape=[jax.ShapeDtypeStruct((nseq * tlen, D_MODEL), F32),
                 jax.ShapeDtypeStruct((nseq, N_GROUPS, HEAD, GROUP), F32)],
      scratch_shapes=scratch,
      compiler_params=pltpu.CompilerParams(
          dimension_semantics=("arbitrary", "arbitrary"), vmem_limit_bytes=VMEM_LIMIT),
      name="wkv_chain" if chain else "wkv_step",
  )(*args)


def _mlp_kernel(n_p_tiles, x_ref, mp_ref, ms_ref, wo_ref, g1_ref, b1_ref, w1_ref, w2_ref, g2_ref, b2_ref,
                o_ref, x1_scr, x1b_scr, acc_scr):
  i = pl.program_id(0)
  j = pl.program_id(1)

  @pl.when(j == 0)
  def _():
    merged = jnp.where(i < n_p_tiles, mp_ref[...], ms_ref[...])
    y = DN_ALPHA * x_ref[...] + _dot(merged.astype(BF16), wo_ref[...])
    x1 = _layer_norm(y, g1_ref[...], b1_ref[...])
    x1_scr[...] = x1
    x1b_scr[...] = x1.astype(BF16)
    acc_scr[...] = jnp.zeros_like(acc_scr)

  hid = jnp.square(jnp.maximum(_dot(x1b_scr[...], w1_ref[...]), 0.0))
  acc_scr[...] += _dot(hid.astype(BF16), w2_ref[...])

  @pl.when(j == pl.num_programs(1) - 1)
  def _():
    o_ref[...] = _layer_norm(DN_ALPHA * x1_scr[...] + acc_scr[...], g2_ref[...], b2_ref[...])


def _mlp(x, merged_p, merged_s, prm, l, tm=512, tf=1024):
  n = x.shape[0]
  n_p = merged_p.shape[0] // tm
  n_s = merged_s.shape[0] // tm
  vec = pl.BlockSpec((None, 1, D_MODEL), lambda i, j: (l, 0, 0))
  return pl.pallas_call(
      functools.partial(_mlp_kernel, n_p),
      grid=(n // tm, D_FF // tf),
      in_specs=[
          pl.BlockSpec((tm, D_MODEL), lambda i, j: (i, 0)),
          pl.BlockSpec((tm, D_MODEL), lambda i, j: (jnp.minimum(i, n_p - 1), 0)),
          pl.BlockSpec((tm, D_MODEL), lambda i, j: (jnp.clip(i - n_p, 0, n_s - 1), 0)),
          pl.BlockSpec((None, D_MODEL, D_MODEL), lambda i, j: (l, 0, 0)),
          vec, vec,
          pl.BlockSpec((None, D_MODEL, tf), lambda i, j: (l, 0, j)),
          pl.BlockSpec((None, tf, D_MODEL), lambda i, j: (l, j, 0)),
          vec, vec,
      ],
      out_specs=pl.BlockSpec((tm, D_MODEL), lambda i, j: (i, 0)),
      out_shape=jax.ShapeDtypeStruct((n, D_MODEL), F32),
      scratch_shapes=[pltpu.VMEM((tm, D_MODEL), F32), pltpu.VMEM((tm, D_MODEL), BF16),
                      pltpu.VMEM((tm, D_MODEL), F32)],
      compiler_params=pltpu.CompilerParams(
          dimension_semantics=("arbitrary", "arbitrary"), vmem_limit_bytes=VMEM_LIMIT),
      name="outproj_mlp",
  )(x, merged_p, merged_s, prm["w_out"], prm["ln1_g"], prm["ln1_b"], prm["mlp_w1"], prm["mlp_w2"],
    prm["ln2_g"], prm["ln2_b"])


def _block_diag4(w):
  depth = w.shape[0]
  w5 = w.reshape(depth, N_GROUPS, 4, HEAD, HEAD)
  eye = jnp.eye(4, dtype=w.dtype)
  return jnp.einsum("lgaij,ab->lgaibj", w5, eye).reshape(depth, N_GROUPS, GROUP, GROUP)


def _state_to_cat(s):
  b = s.shape[0]
  return s.reshape(b, N_GROUPS, 4, HEAD, HEAD).transpose(0, 1, 4, 2, 3).reshape(b, N_GROUPS, HEAD, GROUP)


def _cat_to_state(z):
  b = z.shape[0]
  return z.reshape(b, N_GROUPS, HEAD, 4, HEAD).transpose(0, 1, 3, 4, 2).reshape(b, 4 * N_GROUPS, HEAD, HEAD)


def kernel(x_prompt, x_sample, state_conv, state_lru, state_shift, state_wkv, w_in, conv_w, conv_b, lru_wa,
           lru_ba, lru_wx, lru_bx, lru_a_param, shift_mu, decay_up, w0, aaa_up, a0, gate_up, k_k, k_a, r_k,
           gn_w, gn_b, w_out, ln1_g, ln1_b, mlp_w1, mlp_w2, ln2_g, ln2_b):
  bp, tp, _ = x_prompt.shape
  bs, ts, _ = x_sample.shape
  depth = w_in.shape[0]
  n_p = bp * tp
  n_s = bs * ts

  row = lambda p: p.reshape(depth, 1, -1)
  w_in_b = w_in.astype(BF16)
  zero = jnp.zeros((depth, 64, D_MODEL), F32)
  lora_w = jnp.concatenate([jnp.concatenate([decay_up, zero], axis=2),
                            jnp.concatenate([zero, aaa_up], axis=2)], axis=1).astype(BF16)
  prm = dict(
      conv_w=conv_w, conv_b=row(conv_b),
      wa_bd=_block_diag4(lru_wa).astype(BF16), wx_bd=_block_diag4(lru_wx).astype(BF16),
      lru_ba=row(lru_ba), lru_bx=row(lru_bx), lru_a_param=row(lru_a_param),
      shift_mu=row(shift_mu), lora_w=lora_w, w0=row(w0), a0=row(a0), gate_up=gate_up.astype(BF16),
      k_k=row(k_k), k_a=row(k_a), r_k=row(r_k), gn_w=row(gn_w), gn_b=row(gn_b),
      w_out=w_out.astype(BF16), ln1_g=row(ln1_g), ln1_b=row(ln1_b),
      mlp_w1=mlp_w1.astype(BF16), mlp_w2=mlp_w2.astype(BF16), ln2_g=row(ln2_g), ln2_b=row(ln2_b),
  )
  w_xy = w_in_b[:, :, :2 * D_MODEL]
  w_sh = w_in_b[:, :, 2 * D_MODEL:2 * D_MODEL + SHIFT_WIDTH]
  w_gt = w_in_b[:, :, 2 * D_MODEL + SHIFT_WIDTH:]

  x = jnp.concatenate([x_prompt.reshape(n_p, D_MODEL), x_sample.reshape(n_s, D_MODEL)], axis=0)
  conv_p, lru_p, shift_p, wkv_p = [], [], [], []
  conv_s, lru_s, shift_s, wkv_s = [], [], [], []
  for l in range(depth):
    xy = _inproj(x, w_xy, l)
    sh = _inproj(x, w_sh, l)
    gt = _inproj(x, w_gt, l)

    pa_p, hl_p = _lru(xy, gt, prm, l, chain=True, nseq=bp, tlen=tp, row0=0)
    prev_s = jnp.pad(state_conv[l], ((0, 0), (5, 0), (0, 0))).reshape(bs * 8, D_MODEL)
    pa_s, hl_s = _lru(xy, gt, prm, l, chain=False, nseq=bs, tlen=ts, row0=n_p,
                      prev=prev_s, h0=state_lru[l].reshape(bs, 1, D_MODEL))

    mg_p, z_p = _wkv(sh, gt, pa_p, prm, l, chain=True, nseq=bp, tlen=tp, row0=0, cps=2)
    sprev = jnp.repeat(state_shift[l], ts, axis=0)
    mg_s, z_s = _wkv(sh, gt, pa_s, prm, l, chain=False, nseq=bs, tlen=ts, row0=n_p, cps=2,
                     sprev=sprev, s0=_state_to_cat(state_wkv[l]))

    x = _mlp(x, mg_p, mg_s, prm, l)

    lx_p = xy[:n_p].reshape(bp, tp, 2 * D_MODEL)
    lx_s = xy[n_p:].reshape(bs, ts, 2 * D_MODEL)
    conv_p.append(lx_p[:, tp - 3:, :D_MODEL])
    conv_s.append(lx_s[:, ts - 3:, :D_MODEL])
    lru_p.append(hl_p.reshape(bp, D_MODEL))
    lru_s.append(hl_s.reshape(bs, D_MODEL))
    shift_p.append(sh[:n_p].reshape(bp, tp, SHIFT_WIDTH)[:, tp - 1])
    shift_s.append(sh[n_p:].reshape(bs, ts, SHIFT_WIDTH)[:, ts - 1])
    wkv_p.append(_cat_to_state(z_p))
    wkv_s.append(_cat_to_state(z_s))

  return (x[:n_p].reshape(bp, tp, D_MODEL), x[n_p:].reshape(bs, ts, D_MODEL),
          jnp.stack(conv_p), jnp.stack(lru_p), jnp.stack(shift_p), jnp.stack(wkv_p),
          jnp.stack(conv_s), jnp.stack(lru_s), jnp.stack(shift_s), jnp.stack(wkv_s))
```

```python
import functools
import math

import jax
import jax.numpy as jnp
from jax import lax
from jax.experimental import pallas as pl
from jax.experimental.pallas import tpu as pltpu

F32 = jnp.float32
BF16 = jnp.bfloat16

D_MODEL = 1024
DEPTH = 4
LRU_C = 8.0
HEAD = 64
GROUP = 256
LANES = 128
N_GROUPS = D_MODEL // GROUP
SHIFT_WIDTH = 3 * D_MODEL + 64 + 64 + 128
XY_WIDTH = 2 * D_MODEL
IN_COLS = 2 * XY_WIDTH + SHIFT_WIDTH
D_FF = 4 * D_MODEL
DN_ALPHA = (2 * DEPTH) ** 0.25
LN_EPS = 1e-5
GN_EPS = 64e-5
CHUNK = 64
VMEM_LIMIT = 56 * 1024 * 1024
NP_GRAM, NP_INV, NP_STATE = 1, 1, 1
NP_CUM = 2


def _split(x, n):
  pieces = []
  rem = x
  for i in range(n):
    p = rem.astype(BF16)
    pieces.append(p)
    if i + 1 < n:
      rem = rem - p.astype(F32)
  return pieces


def _dot(a, b, trans_b=False):
  dims = (((1,), (1,)), ((), ())) if trans_b else (((1,), (0,)), ((), ()))
  return lax.dot_general(a, b, dims, preferred_element_type=F32)


def _mm(a, b, na=1, nb=1, trans_b=False):
  ap = a if isinstance(a, list) else ([a] if a.dtype == BF16 else _split(a, na))
  bp = b if isinstance(b, list) else ([b] if b.dtype == BF16 else _split(b, nb))
  order = max(len(ap), len(bp))
  acc = None
  for i, x in enumerate(ap):
    for j, y in enumerate(bp):
      if i + j < order:
        t = _dot(x, y, trans_b)
        acc = t if acc is None else acc + t
  return acc


def _softplus(x):
  return jnp.maximum(x, 0.0) + jnp.log1p(jnp.exp(-jnp.abs(x)))


def _expm1(x):
  u = jnp.exp(x)
  lu = jnp.log(u)
  small = jnp.where(u == 1.0, x, (u - 1.0) * x / jnp.where(u == 1.0, 1.0, lu))
  return jnp.where(jnp.abs(x) < 0.5, small, u - 1.0)


def _gelu_tanh(x):
  c = math.sqrt(2.0 / math.pi)
  return x * (0.5 * (1.0 + jnp.tanh(c * (x + 0.044715 * (x * x * x)))))


def _layer_norm(x, g, b):
  mu = jnp.mean(x, axis=-1, keepdims=True)
  xc = x - mu
  var = jnp.mean(xc * xc, axis=-1, keepdims=True)
  return xc * lax.rsqrt(var + LN_EPS) * g + b


def _iota(shape, dim):
  return lax.broadcasted_iota(jnp.int32, shape, dim)


def _cat(xs, axis):
  return xs[0] if len(xs) == 1 else jnp.concatenate(xs, axis=axis)


def _params(sem):
  return pltpu.CompilerParams(dimension_semantics=sem, vmem_limit_bytes=VMEM_LIMIT)


def _inproj_kernel(x_ref, w_ref, xy_ref, sh_ref, gt_ref):
  xb = x_ref[...].astype(BF16)
  xy_ref[...] = _dot(xb, w_ref[:, 0:XY_WIDTH])
  sh_ref[...] = _dot(xb, w_ref[:, XY_WIDTH:XY_WIDTH + SHIFT_WIDTH])
  gt_ref[...] = _dot(xb, w_ref[:, XY_WIDTH + SHIFT_WIDTH:IN_COLS])


def _inproj(x, w_in_b, l, tm=256):
  n = x.shape[0]
  row = lambda width: pl.BlockSpec((tm, width), lambda i: (i, 0))
  return pl.pallas_call(
      _inproj_kernel,
      grid=(n // tm,),
      in_specs=[row(D_MODEL), pl.BlockSpec((None, D_MODEL, IN_COLS), lambda i: (l, 0, 0))],
      out_specs=[row(XY_WIDTH), row(SHIFT_WIDTH), row(XY_WIDTH)],
      out_shape=[jax.ShapeDtypeStruct((n, XY_WIDTH), F32), jax.ShapeDtypeStruct((n, SHIFT_WIDTH), F32),
                 jax.ShapeDtypeStruct((n, XY_WIDTH), F32)],
      compiler_params=_params(("arbitrary",)),
      name="inproj",
  )(x, w_in_b)


def _lru_kernel(chain, *refs):
  if chain:
    (x_ref, y_ref, ga_ref, cw_ref, cb_ref, wa_ref, ba_ref, wx_ref, bx_ref, ap_ref,
     pa_ref, hl_ref, ct_ref, cx_scr, ch_scr) = refs
  else:
    (x_ref, y_ref, ga_ref, prev_ref, h0_ref, cw_ref, cb_ref, wa_ref, ba_ref, wx_ref, bx_ref,
     ap_ref, pa_ref, hl_ref) = refs
  rows = x_ref.shape[0]
  nt = rows // 8
  x = x_ref[...]
  x3 = x.reshape(nt, 8, D_MODEL)
  if chain:
    c = pl.program_id(1)

    @pl.when(c == 0)
    def _():
      cx_scr[...] = jnp.zeros_like(cx_scr)
      ch_scr[...] = jnp.zeros_like(ch_scr)

    prev3 = jnp.concatenate([cx_scr[...].reshape(1, 8, D_MODEL), x3[:nt - 1]], axis=0)
    cx_scr[...] = x_ref[rows - 8:rows, :]
    ct_ref[0] = x_ref[rows - 8:rows, :]
  else:
    prev3 = prev_ref[...].reshape(nt, 8, D_MODEL)

  sub = _iota((nt, 8, D_MODEL), 1)
  u = cb_ref[...] + cw_ref[3:4, :] * x
  for j in (1, 2, 3):
    xs = jnp.where(sub >= j, pltpu.roll(x3, j, 1), pltpu.roll(prev3, j, 1))
    u = u + cw_ref[3 - j:4 - j, :] * xs.reshape(rows, D_MODEL)

  gr, gi = [], []
  for g in range(N_GROUPS):
    ug = u[:, g * GROUP:(g + 1) * GROUP].astype(BF16)
    gr.append(_dot(ug, wa_ref[g]))
    gi.append(_dot(ug, wx_ref[g]))
  gate_r = jax.nn.sigmoid(jnp.concatenate(gr, axis=1) + ba_ref[...])
  gate_i = jax.nn.sigmoid(jnp.concatenate(gi, axis=1) + bx_ref[...])
  log_a = (-LRU_C * gate_r) * _softplus(ap_ref[...])
  a = jnp.exp(log_a)
  mult = jnp.sqrt(-_expm1(2.0 * log_a))
  if chain:
    first = jnp.logical_and(_iota((rows, D_MODEL), 0) == 0, c == 0)
    mult = jnp.where(first, 1.0, mult)
  xin = u * gate_i * mult

  a3 = a.reshape(nt, 8, D_MODEL)
  b3 = xin.reshape(nt, 8, D_MODEL)
  for s in (1, 2, 4):
    m = sub >= s
    b3 = jnp.where(m, a3 * pltpu.roll(b3, s, 1) + b3, b3)
    a3 = jnp.where(m, a3 * pltpu.roll(a3, s, 1), a3)
  if chain:
    hc = ch_scr[7:8, :]
    hs = []
    for i in range(nt):
      hi = a3[i] * hc + b3[i]
      hs.append(hi)
      hc = hi[7:8, :]
    h = jnp.concatenate(hs, axis=0)
    ch_scr[...] = hs[-1]
    hl_ref[0] = hc
  else:
    h3 = a3 * h0_ref[...] + b3
    hl_ref[...] = h3[:, 7:8, :]
    h = h3.reshape(rows, D_MODEL)

  pa_ref[...] = jax.nn.sigmoid(ga_ref[...]) * (h * _gelu_tanh(y_ref[...]))


def _lru(xy, gates, prm, l, *, chain, nseq, tlen, prev=None, h0=None):
  if chain:
    rows = 256
    nstep = tlen // rows
    grid = (nseq, nstep)
    rmap = lambda b, c: (b * nstep + c, 0)
    hl_spec = pl.BlockSpec((1, 1, D_MODEL), lambda b, c: (b, 0, 0))
  else:
    nb = 16
    rows = nb * tlen
    grid = (nseq // nb, 1)
    rmap = lambda b, c: (b, 0)
    hl_spec = pl.BlockSpec((nb, 1, D_MODEL), lambda b, c: (b, 0, 0))
  rmap1 = lambda b, c: (rmap(b, c)[0], 1)
  vec = pl.BlockSpec((None, 1, D_MODEL), lambda b, c: (l, 0, 0))
  gates_w = pl.BlockSpec((None, N_GROUPS, GROUP, GROUP), lambda b, c: (l, 0, 0, 0))
  in_specs = [pl.BlockSpec((rows, D_MODEL), rmap),
              pl.BlockSpec((rows, D_MODEL), rmap1),
              pl.BlockSpec((rows, D_MODEL), rmap)]
  args = [xy, xy, gates]
  if not chain:
    in_specs += [pl.BlockSpec((rows, D_MODEL), rmap),
                 pl.BlockSpec((nb, 1, D_MODEL), lambda b, c: (b, 0, 0))]
    args += [prev, h0]
  in_specs += [pl.BlockSpec((None, 4, D_MODEL), lambda b, c: (l, 0, 0)), vec, gates_w, vec, gates_w, vec, vec]
  args += [prm["conv_w"], prm["conv_b"], prm["wa_bd"], prm["lru_ba"], prm["wx_bd"], prm["lru_bx"],
           prm["lru_a_param"]]
  out_specs = [pl.BlockSpec((rows, D_MODEL), rmap), hl_spec]
  out_shape = [jax.ShapeDtypeStruct((nseq * tlen, D_MODEL), F32), jax.ShapeDtypeStruct((nseq, 1, D_MODEL), F32)]
  scratch = []
  if chain:
    out_specs.append(pl.BlockSpec((1, 8, D_MODEL), lambda b, c: (b, 0, 0)))
    out_shape.append(jax.ShapeDtypeStruct((nseq, 8, D_MODEL), F32))
    scratch = [pltpu.VMEM((8, D_MODEL), F32), pltpu.VMEM((8, D_MODEL), F32)]
  return pl.pallas_call(
      functools.partial(_lru_kernel, chain),
      grid=grid, in_specs=in_specs, out_specs=out_specs, out_shape=out_shape, scratch_shapes=scratch,
      compiler_params=_params(("arbitrary", "arbitrary")),
      name="lru_chain" if chain else "lru_step",
  )(*args)


def _wkv_masks(tlen):
  lt = tlen.bit_length() - 1
  ri = _iota((CHUNK, GROUP), 0)
  cm = _iota((CHUNK, GROUP), 1) & (CHUNK - 1)
  same = (ri >> lt) == (cm >> lt)
  rs = _iota((2 * CHUNK, CHUNK), 0)
  cs = _iota((2 * CHUNK, CHUNK), 1)
  rr = rs & (CHUNK - 1)
  same_s = (rr >> lt) == (cs >> lt)
  cum_mask = jnp.logical_and(same_s, jnp.logical_or(cs <= rr, rs >= CHUNK))
  lane = _iota((CHUNK, LANES), 1)
  half = [jnp.where(lane < HEAD, 1.0, 0.0), jnp.where(lane >= HEAD, 1.0, 0.0)]
  ones_bd = jnp.where((_iota((GROUP, GROUP), 0) >> 6) == (_iota((GROUP, GROUP), 1) >> 6), 1.0, 0.0)
  return dict(
      strict=jnp.logical_and(same, cm < ri),
      incl=jnp.logical_and(same, cm <= ri),
      eye=jnp.where(cm == ri, 1.0, 0.0),
      half=half,
      half_b=[h.astype(BF16) for h in half],
      zero_b=jnp.zeros((CHUNK, LANES), BF16),
      ones_bd=ones_bd.astype(BF16),
      cum=jnp.where(cum_mask, 1.0, 0.0).astype(BF16),
      colseq=(_iota((GROUP, 2 * CHUNK), 1) & (CHUNK - 1)) >> lt,
      lt=lt,
  )


def _bd_from_tiles(tiles, mk):
  z = mk["zero_b"]
  return jnp.concatenate([
      jnp.concatenate([tiles[0], z], axis=1), jnp.concatenate([tiles[1], z], axis=1),
      jnp.concatenate([z, tiles[2]], axis=1), jnp.concatenate([z, tiles[3]], axis=1)], axis=0)


def _bd(x, mk, n=1):
  out = []
  for p in _split(x, n):
    out.append(_bd_from_tiles([p[:, (i // 2) * LANES:(i // 2 + 1) * LANES] * mk["half_b"][i % 2]
                               for i in range(4)], mk))
  return out


def _headsum(x, mk):
  return _mm(x, mk["ones_bd"])


def _wkv_kernel(chain, tlen, cps, *refs):
  if chain:
    (sh_ref, gb_ref, pa_ref, mu_ref, lw_ref, w0_ref, a0_ref, gup_ref, kk_ref, ka_ref, rk_ref,
     gnw_ref, gnb_ref, out_ref, sout_ref, st_ref, s_scr, carry_scr) = refs
  else:
    (sh_ref, gb_ref, pa_ref, sp_ref, s0_ref, mu_ref, lw_ref, w0_ref, a0_ref, gup_ref, kk_ref, ka_ref,
     rk_ref, gnw_ref, gnb_ref, out_ref, sout_ref) = refs
  rows = cps * CHUNK
  nsq = CHUNK // tlen
  mk = _wkv_masks(tlen)
  groups = range(N_GROUPS)
  units = [(ci, gi) for ci in range(cps) for gi in groups]
  gsl = lambda gi: slice(gi * GROUP, (gi + 1) * GROUP)
  cut = lambda x, un: x[un[0] * CHUNK:(un[0] + 1) * CHUNK, gsl(un[1])]
  tile_of = lambda i: slice((i // 2) * LANES, (i // 2 + 1) * LANES)
  rblk = lambda i: slice(i * HEAD, (i + 1) * HEAD)

  sh = sh_ref[...]
  rolled = pltpu.roll(sh, 1, 0)
  row = _iota((rows, SHIFT_WIDTH), 0)
  if chain:
    c = pl.program_id(1)

    @pl.when(c == 0)
    def _():
      carry_scr[...] = jnp.zeros_like(carry_scr)
      s_scr[...] = jnp.zeros_like(s_scr)

    prev = jnp.where(row == 0, carry_scr[7:8, :], rolled)
    carry_scr[...] = sh_ref[rows - 8:rows, :]
    st_ref[0] = sh_ref[rows - 8:rows, :]
  else:
    prev = jnp.where((row & (tlen - 1)) == 0, sp_ref[...], rolled)
  mixed = sh + (prev - sh) * mu_ref[...]
  r = mixed[:, 0:D_MODEL]
  k = mixed[:, D_MODEL:2 * D_MODEL]
  v = mixed[:, 2 * D_MODEL:3 * D_MODEL]
  xl = mixed[:, 3 * D_MODEL:3 * D_MODEL + 128]
  xg = mixed[:, 3 * D_MODEL + 128:SHIFT_WIDTH]
  lane = _iota((rows, 128), 1)
  lin = jnp.where(lane < 64, jnp.tanh(xl), xl).astype(BF16)
  lo = _dot(lin, lw_ref[...])
  w_log = -_softplus(-(w0_ref[...] + lo[:, :D_MODEL])) - 0.5
  w = -jnp.exp(w_log)
  a = jax.nn.sigmoid(a0_ref[...] + lo[:, D_MODEL:])
  g = _dot(jax.nn.sigmoid(xg).astype(BF16), gup_ref[...])
  kk = k * kk_ref[...]
  k_h = k * (1.0 + (a - 1.0) * ka_ref[...])
  rkr = r * k_h * rk_ref[...]
  kkn = _cat([kk[:, gsl(gi)] / jnp.maximum(jnp.sqrt(_headsum(kk[:, gsl(gi)] * kk[:, gsl(gi)], mk)), 1e-12)
              for gi in groups], 1)
  av = -kkn
  bv = kkn * a

  cums, tots = [], []
  for ci in range(cps):
    cw = _mm(mk["cum"], w[ci * CHUNK:(ci + 1) * CHUNK], nb=NP_CUM)
    cums.append(cw[:CHUNK])
    tots.append(cw[CHUNK:])
  cum = _cat(cums, 0)
  tot = _cat(tots, 0)
  at = av * jnp.exp(cum - w)
  rt = r * jnp.exp(cum)
  e_neg = jnp.exp(-cum)
  bt = bv * e_neg
  kt = k_h * e_neg
  e_rem = jnp.exp(tot - cum)
  bh = bv * e_rem
  kh = k_h * e_rem
  d_tot = jnp.exp(tot)

  lhs, n_ab, a_ak, a_rbk = {}, {}, {}, {}
  for un in units:
    lhs[un] = jnp.concatenate([cut(at, un), cut(rt, un)], axis=0)
    gb = _mm(lhs[un], _bd(cut(bt, un), mk, NP_GRAM), NP_GRAM, trans_b=True)
    gk = _mm(lhs[un], _bd(cut(kt, un), mk, NP_GRAM), NP_GRAM, trans_b=True)
    n_ab[un] = jnp.where(mk["strict"], gb[:CHUNK], 0.0)
    a_ak[un] = jnp.where(mk["strict"], gk[:CHUNK], 0.0)
    a_rbk[un] = jnp.concatenate([jnp.where(mk["incl"], gb[CHUNK:], 0.0),
                                 jnp.where(mk["incl"], gk[CHUNK:], 0.0)], axis=1)

  t = {un: mk["eye"] + n_ab[un] for un in units}
  q = {un: _mm(n_ab[un], _bd(n_ab[un], mk, NP_INV), NP_INV) for un in units}
  levels = mk["lt"]
  for lvl in range(1, levels):
    for un in units:
      bq = _bd(q[un], mk, NP_INV)
      if lvl + 1 < levels:
        tq = _mm(jnp.concatenate([t[un], q[un]], axis=0), bq, NP_INV)
        t[un] = t[un] + tq[:CHUNK]
        q[un] = tq[CHUNK:]
      else:
        t[un] = t[un] + _mm(t[un], bq, NP_INV)

  bdv, a_v, xt, dcol = {}, {}, {}, {}
  for un in units:
    bdv[un] = _bd(cut(v, un), mk, NP_STATE)
    a_v[un] = _mm(a_ak[un], bdv[un], NP_STATE)
    dt = cut(d_tot, un)
    xpt = jnp.concatenate([cut(bh, un), cut(kh, un), dt, dt], axis=0).T
    xt[un] = xpt[:, :2 * CHUNK]
    dcol[un] = xpt[:, 2 * CHUNK:]

  if chain:
    p_cur = {gi: [s_scr[gi, i] for i in range(4)] for gi in groups}
  o = {}
  for ci in range(cps):
    states, a_p, r_p, u = {}, {}, {}, {}
    for gi in groups:
      un = (ci, gi)
      if chain:
        states[gi] = [p_cur[gi]]
      else:
        states[gi] = [[s0_ref[ci * nsq + j, gi][:, tile_of(i)] * mk["half"][i % 2] for i in range(4)]
                      for j in range(nsq)]
      sbd = [_bd_from_tiles([tl.astype(BF16) for tl in st], mk) for st in states[gi]]
      if nsq == 1:
        lp = _mm(lhs[un], sbd[0])
        a_p[gi], r_p[gi] = lp[:CHUNK], lp[CHUNK:]
      else:
        aps, rps = [], []
        for j in range(nsq):
          lj = jnp.concatenate([lhs[un][j * tlen:(j + 1) * tlen],
                                lhs[un][CHUNK + j * tlen:CHUNK + (j + 1) * tlen]], axis=0)
          lpj = _mm(lj, sbd[j])
          aps.append(lpj[:tlen])
          rps.append(lpj[tlen:])
        a_p[gi] = jnp.concatenate(aps, axis=0)
        r_p[gi] = jnp.concatenate(rps, axis=0)
    for gi in groups:
      un = (ci, gi)
      u[gi] = _mm(t[un], _bd(a_p[gi] + a_v[un], mk, NP_STATE), NP_STATE)
    for gi in groups:
      un = (ci, gi)
      y = jnp.concatenate([u[gi], cut(v, un)], axis=0)
      for j in range(nsq):
        xtj = xt[un] if nsq == 1 else jnp.where(mk["colseq"] == j, xt[un], 0.0)
        upd = _mm(xtj, y, NP_STATE, NP_STATE)
        new = []
        for i in range(4):
          if nsq == 1:
            dcb = dcol[un][rblk(i)]
          else:
            dcb = jnp.broadcast_to(dcol[un][rblk(i), j * tlen:j * tlen + 1], (HEAD, LANES))
          new.append((states[gi][j][i] * dcb + upd[rblk(i), tile_of(i)]) * mk["half"][i % 2])
        if chain:
          p_cur[gi] = new
        else:
          sout_ref[ci * nsq + j, gi] = jnp.concatenate([new[0] + new[1], new[2] + new[3]], axis=1)
    for gi in groups:
      un = (ci, gi)
      bdu = _bd(u[gi], mk, NP_STATE)
      o[un] = r_p[gi] + _mm(a_rbk[un], [jnp.concatenate([pu, pv], axis=0) for pu, pv in zip(bdu, bdv[un])],
                            NP_STATE)
  if chain:
    for gi in groups:
      for i in range(4):
        s_scr[gi, i] = p_cur[gi][i]

    @pl.when(c == pl.num_programs(1) - 1)
    def _():
      for gi in groups:
        sout_ref[0, gi] = jnp.concatenate([s_scr[gi, 0] + s_scr[gi, 1], s_scr[gi, 2] + s_scr[gi, 3]], axis=1)

  outs = []
  for gi in groups:
    og = _cat([o[(ci, gi)] for ci in range(cps)], 0)
    o_mu = _headsum(og, mk) * (1.0 / HEAD)
    oc = og - o_mu
    o_var = _headsum(oc * oc, mk) * (1.0 / HEAD)
    o_n = oc * lax.rsqrt(o_var + GN_EPS) * gnw_ref[:, gsl(gi)] + gnb_ref[:, gsl(gi)]
    bonus = _headsum(rkr[:, gsl(gi)], mk) * v[:, gsl(gi)]
    outs.append((o_n + bonus) * g[:, gsl(gi)])
  out_ref[...] = pa_ref[...] + jax.nn.sigmoid(gb_ref[...]) * jnp.concatenate(outs, axis=1)


def _wkv(sh, gates, pa, prm, l, *, chain, nseq, tlen, cps, sprev=None, s0=None):
  rows = cps * CHUNK
  if chain:
    nstep = tlen // rows
    grid = (nseq, nstep)
    rmap = lambda b, c: (b * nstep + c, 0)
    sblk = 1
    ktlen = CHUNK
  else:
    sblk = rows // tlen
    grid = (nseq // sblk, 1)
    rmap = lambda b, c: (b, 0)
    ktlen = tlen
  rmap1 = lambda b, c: (rmap(b, c)[0], 1)
  smap = lambda b, c: (b, 0, 0, 0)
  vec = pl.BlockSpec((None, 1, D_MODEL), lambda b, c: (l, 0, 0))
  in_specs = [pl.BlockSpec((rows, SHIFT_WIDTH), rmap),
              pl.BlockSpec((rows, D_MODEL), rmap1),
              pl.BlockSpec((rows, D_MODEL), rmap)]
  args = [sh, gates, pa]
  if not chain:
    in_specs += [pl.BlockSpec((rows, SHIFT_WIDTH), rmap),
                 pl.BlockSpec((sblk, N_GROUPS, HEAD, GROUP), smap)]
    args += [sprev, s0]
  in_specs += [pl.BlockSpec((None, 1, SHIFT_WIDTH), lambda b, c: (l, 0, 0)),
               pl.BlockSpec((None, 128, 2 * D_MODEL), lambda b, c: (l, 0, 0)),
               vec, vec,
               pl.BlockSpec((None, 128, D_MODEL), lambda b, c: (l, 0, 0)),
               vec, vec, vec, vec, vec]
  args += [prm["shift_mu"], prm["lora_w"], prm["w0"], prm["a0"], prm["gate_up"], prm["k_k"], prm["k_a"],
           prm["r_k"], prm["gn_w"], prm["gn_b"]]
  out_specs = [pl.BlockSpec((rows, D_MODEL), rmap), pl.BlockSpec((sblk, N_GROUPS, HEAD, GROUP), smap)]
  out_shape = [jax.ShapeDtypeStruct((nseq * tlen, D_MODEL), F32),
               jax.ShapeDtypeStruct((nseq, N_GROUPS, HEAD, GROUP), F32)]
  scratch = []
  if chain:
    out_specs.append(pl.BlockSpec((1, 8, SHIFT_WIDTH), lambda b, c: (b, 0, 0)))
    out_shape.append(jax.ShapeDtypeStruct((nseq, 8, SHIFT_WIDTH), F32))
    scratch = [pltpu.VMEM((N_GROUPS, 4, HEAD, LANES), F32), pltpu.VMEM((8, SHIFT_WIDTH), F32)]
  return pl.pallas_call(
      functools.partial(_wkv_kernel, chain, ktlen, cps),
      grid=grid, in_specs=in_specs, out_specs=out_specs, out_shape=out_shape, scratch_shapes=scratch,
      compiler_params=_params(("arbitrary", "arbitrary")),
      name="wkv_chain" if chain else "wkv_step",
  )(*args)


def _mlp_kernel(x_ref, m_ref, wo_ref, g1_ref, b1_ref, w1_ref, w2_ref, g2_ref, b2_ref,
                o_ref, x1_scr, x1b_scr, acc_scr):
  j = pl.program_id(1)

  @pl.when(j == 0)
  def _():
    y = DN_ALPHA * x_ref[...] + _dot(m_ref[...].astype(BF16), wo_ref[...])
    x1 = _layer_norm(y, g1_ref[...], b1_ref[...])
    x1_scr[...] = x1
    x1b_scr[...] = x1.astype(BF16)
    acc_scr[...] = jnp.zeros_like(acc_scr)

  hid = jnp.square(jnp.maximum(_dot(x1b_scr[...], w1_ref[...]), 0.0))
  acc_scr[...] += _dot(hid.astype(BF16), w2_ref[...])

  @pl.when(j == pl.num_programs(1) - 1)
  def _():
    o_ref[...] = _layer_norm(DN_ALPHA * x1_scr[...] + acc_scr[...], g2_ref[...], b2_ref[...])


def _mlp(x, merged, prm, l, tm=512, tf=1024):
  n = x.shape[0]
  vec = pl.BlockSpec((None, 1, D_MODEL), lambda i, j: (l, 0, 0))
  row = pl.BlockSpec((tm, D_MODEL), lambda i, j: (i, 0))
  return pl.pallas_call(
      _mlp_kernel,
      grid=(n // tm, D_FF // tf),
      in_specs=[
          row, row,
          pl.BlockSpec((None, D_MODEL, D_MODEL), lambda i, j: (l, 0, 0)),
          vec, vec,
          pl.BlockSpec((None, D_MODEL, tf), lambda i, j: (l, 0, j)),
          pl.BlockSpec((None, tf, D_MODEL), lambda i, j: (l, j, 0)),
          vec, vec,
      ],
      out_specs=row,
      out_shape=jax.ShapeDtypeStruct((n, D_MODEL), F32),
      scratch_shapes=[pltpu.VMEM((tm, D_MODEL), F32), pltpu.VMEM((tm, D_MODEL), BF16),
                      pltpu.VMEM((tm, D_MODEL), F32)],
      compiler_params=_params(("arbitrary", "arbitrary")),
      name="outproj_mlp",
  )(x, merged, prm["w_out"], prm["ln1_g"], prm["ln1_b"], prm["mlp_w1"], prm["mlp_w2"],
    prm["ln2_g"], prm["ln2_b"])


def _block_diag4(w):
  depth = w.shape[0]
  w5 = w.reshape(depth, N_GROUPS, 4, HEAD, HEAD)
  eye = jnp.eye(4, dtype=w.dtype)
  return jnp.einsum("lgaij,ab->lgaibj", w5, eye).reshape(depth, N_GROUPS, GROUP, GROUP)


def _state_to_cat(s):
  b = s.shape[0]
  return s.reshape(b, N_GROUPS, 4, HEAD, HEAD).transpose(0, 1, 4, 2, 3).reshape(b, N_GROUPS, HEAD, GROUP)


def _cat_to_state(z):
  b = z.shape[0]
  return z.reshape(b, N_GROUPS, HEAD, 4, HEAD).transpose(0, 1, 3, 4, 2).reshape(b, 4 * N_GROUPS, HEAD, HEAD)


def kernel(x_prompt, x_sample, state_conv, state_lru, state_shift, state_wkv, w_in, conv_w, conv_b, lru_wa,
           lru_ba, lru_wx, lru_bx, lru_a_param, shift_mu, decay_up, w0, aaa_up, a0, gate_up, k_k, k_a, r_k,
           gn_w, gn_b, w_out, ln1_g, ln1_b, mlp_w1, mlp_w2, ln2_g, ln2_b):
  bp, tp, _ = x_prompt.shape
  bs, ts, _ = x_sample.shape
  depth = w_in.shape[0]

  row = lambda p: p.reshape(depth, 1, -1)
  w_in_b = w_in.astype(BF16)
  zero = jnp.zeros((depth, 64, D_MODEL), F32)
  lora_w = jnp.concatenate([jnp.concatenate([decay_up, zero], axis=2),
                            jnp.concatenate([zero, aaa_up], axis=2)], axis=1).astype(BF16)
  prm = dict(
      conv_w=conv_w, conv_b=row(conv_b),
      wa_bd=_block_diag4(lru_wa).astype(BF16), wx_bd=_block_diag4(lru_wx).astype(BF16),
      lru_ba=row(lru_ba), lru_bx=row(lru_bx), lru_a_param=row(lru_a_param),
      shift_mu=row(shift_mu), lora_w=lora_w, w0=row(w0), a0=row(a0), gate_up=gate_up.astype(BF16),
      k_k=row(k_k), k_a=row(k_a), r_k=row(r_k), gn_w=row(gn_w), gn_b=row(gn_b),
      w_out=w_out.astype(BF16), ln1_g=row(ln1_g), ln1_b=row(ln1_b),
      mlp_w1=mlp_w1.astype(BF16), mlp_w2=mlp_w2.astype(BF16), ln2_g=row(ln2_g), ln2_b=row(ln2_b),
  )

  xp = x_prompt.reshape(bp * tp, D_MODEL)
  xs = x_sample.reshape(bs * ts, D_MODEL)
  conv_p, lru_p, shift_p, wkv_p = [], [], [], []
  conv_s, lru_s, shift_s, wkv_s = [], [], [], []
  for l in range(depth):
    xy, sh, gt = _inproj(xp, w_in_b, l)
    pa, hl, ctail = _lru(xy, gt, prm, l, chain=True, nseq=bp, tlen=tp)
    mg, z, stail = _wkv(sh, gt, pa, prm, l, chain=True, nseq=bp, tlen=tp, cps=2)
    xp = _mlp(xp, mg, prm, l)
    conv_p.append(ctail[:, 5:, :])
    lru_p.append(hl.reshape(bp, D_MODEL))
    shift_p.append(stail[:, 7, :])
    wkv_p.append(_cat_to_state(z))

    xy, sh, gt = _inproj(xs, w_in_b, l)
    prev = jnp.pad(state_conv[l], ((0, 0), (5, 0), (0, 0))).reshape(bs * 8, D_MODEL)
    pa, hl = _lru(xy, gt, prm, l, chain=False, nseq=bs, tlen=ts, prev=prev,
                  h0=state_lru[l].reshape(bs, 1, D_MODEL))
    mg, z = _wkv(sh, gt, pa, prm, l, chain=False, nseq=bs, tlen=ts, cps=2,
                 sprev=jnp.repeat(state_shift[l], ts, axis=0), s0=_state_to_cat(state_wkv[l]))
    xs = _mlp(xs, mg, prm, l)
    conv_s.append(xy.reshape(bs, ts, XY_WIDTH)[:, ts - 3:, :D_MODEL])
    lru_s.append(hl.reshape(bs, D_MODEL))
    shift_s.append(sh.reshape(bs, ts, SHIFT_WIDTH)[:, ts - 1])
    wkv_s.append(_cat_to_state(z))

  return (xp.reshape(bp, tp, D_MODEL), xs.reshape(bs, ts, D_MODEL),
          jnp.stack(conv_p), jnp.stack(lru_p), jnp.stack(shift_p), jnp.stack(wkv_p),
          jnp.stack(conv_s), jnp.stack(lru_s), jnp.stack(shift_s), jnp.stack(wkv_s))
```

```python
import functools
import math

import jax
import jax.numpy as jnp
from jax import lax
from jax.experimental import pallas as pl
from jax.experimental.pallas import tpu as pltpu

F32 = jnp.float32
BF16 = jnp.bfloat16

D_MODEL = 1024
DEPTH = 4
LRU_C = 8.0
HEAD = 64
GROUP = 256
LANES = 128
N_GROUPS = D_MODEL // GROUP
SHIFT_WIDTH = 3 * D_MODEL + 64 + 64 + 128
XY_WIDTH = 2 * D_MODEL
IN_COLS = 2 * XY_WIDTH + SHIFT_WIDTH
D_FF = 4 * D_MODEL
DN_ALPHA = (2 * DEPTH) ** 0.25
LN_EPS = 1e-5
GN_EPS = 64e-5
CHUNK = 64
VMEM_LIMIT = 56 * 1024 * 1024
NP_CUM = 2
N_WKV_OPS = 7


def _split(x, n):
  pieces = []
  rem = x
  for i in range(n):
    p = rem.astype(BF16)
    pieces.append(p)
    if i + 1 < n:
      rem = rem - p.astype(F32)
  return pieces


def _dot(a, b, trans_b=False):
  dims = (((1,), (1,)), ((), ())) if trans_b else (((1,), (0,)), ((), ()))
  return lax.dot_general(a.astype(BF16), b.astype(BF16), dims, preferred_element_type=F32)


def _softplus(x):
  return jnp.maximum(x, 0.0) + jnp.log1p(jnp.exp(-jnp.abs(x)))


def _expm1(x):
  u = jnp.exp(x)
  lu = jnp.log(u)
  small = jnp.where(u == 1.0, x, (u - 1.0) * x / jnp.where(u == 1.0, 1.0, lu))
  return jnp.where(jnp.abs(x) < 0.5, small, u - 1.0)


def _gelu_tanh(x):
  c = math.sqrt(2.0 / math.pi)
  return x * (0.5 * (1.0 + jnp.tanh(c * (x + 0.044715 * (x * x * x)))))


def _layer_norm(x, g, b):
  mu = jnp.mean(x, axis=-1, keepdims=True)
  xc = x - mu
  var = jnp.mean(xc * xc, axis=-1, keepdims=True)
  return xc * lax.rsqrt(var + LN_EPS) * g + b


def _iota(shape, dim):
  return lax.broadcasted_iota(jnp.int32, shape, dim)


def _cat(xs, axis):
  return xs[0] if len(xs) == 1 else jnp.concatenate(xs, axis=axis)


def _params(sem):
  return pltpu.CompilerParams(dimension_semantics=sem, vmem_limit_bytes=VMEM_LIMIT)


def _ones_bd():
  same = (_iota((GROUP, GROUP), 0) >> 6) == (_iota((GROUP, GROUP), 1) >> 6)
  return jnp.where(same, 1.0, 0.0).astype(BF16)


def _gsl(gi):
  return slice(gi * GROUP, (gi + 1) * GROUP)


def _front_kernel(chain, tlen, *refs):
  if chain:
    (x_ref, w_ref, mu_ref, lw_ref, w0_ref, a0_ref, gup_ref, kk_ref, ka_ref, rk_ref,
     xy_ref, ga_ref, ops_ref, dt_ref, gg_ref, gbon_ref, st_ref, carry_scr) = refs
  else:
    (x_ref, w_ref, sp_ref, mu_ref, lw_ref, w0_ref, a0_ref, gup_ref, kk_ref, ka_ref, rk_ref,
     xy_ref, ga_ref, ops_ref, dt_ref, gg_ref, gbon_ref, sh_ref) = refs
  rows = x_ref.shape[0]
  nchunk = rows // CHUNK
  lt = tlen.bit_length() - 1
  ones_bd = _ones_bd()
  headsum = lambda z: _dot(z, ones_bd)

  xb = x_ref[...].astype(BF16)
  sh = _dot(xb, w_ref[:, XY_WIDTH:XY_WIDTH + SHIFT_WIDTH])
  xy_ref[...] = _dot(xb, w_ref[:, 0:XY_WIDTH])
  gt = _dot(xb, w_ref[:, XY_WIDTH + SHIFT_WIDTH:IN_COLS])
  ga_ref[...] = gt[:, :D_MODEL]
  gate_b = gt[:, D_MODEL:]

  rolled = pltpu.roll(sh, 1, 0)
  row = _iota((rows, SHIFT_WIDTH), 0)
  if chain:
    @pl.when(pl.program_id(1) == 0)
    def _():
      carry_scr[...] = jnp.zeros_like(carry_scr)

    prev = jnp.where(row == 0, carry_scr[7:8, :], rolled)
    tail = sh[rows - 8:rows, :]
    carry_scr[...] = tail
    st_ref[0] = tail
  else:
    prev = jnp.where((row & (tlen - 1)) == 0, sp_ref[...], rolled)
    sh_ref[...] = sh
  mixed = sh + (prev - sh) * mu_ref[...]
  r = mixed[:, 0:D_MODEL]
  k = mixed[:, D_MODEL:2 * D_MODEL]
  v = mixed[:, 2 * D_MODEL:3 * D_MODEL]
  xl = mixed[:, 3 * D_MODEL:3 * D_MODEL + 128]
  xg = mixed[:, 3 * D_MODEL + 128:SHIFT_WIDTH]
  lane = _iota((rows, 128), 1)
  lo = _dot(jnp.where(lane < 64, jnp.tanh(xl), xl), lw_ref[...])
  w_log = -_softplus(-(w0_ref[...] + lo[:, :D_MODEL])) - 0.5
  w = -jnp.exp(w_log)
  a = jax.nn.sigmoid(a0_ref[...] + lo[:, D_MODEL:])
  g = _dot(jax.nn.sigmoid(xg), gup_ref[...])
  kk = k * kk_ref[...]
  k_h = k * (1.0 + (a - 1.0) * ka_ref[...])
  rkr = r * k_h * rk_ref[...]
  kk2 = kk * kk
  ss = _cat([headsum(kk2[:, _gsl(gi)]) for gi in range(N_GROUPS)], 1)
  kkn = kk / jnp.maximum(jnp.sqrt(ss), 1e-12)
  bv = kkn * a
  bonus = _cat([headsum(rkr[:, _gsl(gi)]) for gi in range(N_GROUPS)], 1) * v
  gg = jax.nn.sigmoid(gate_b) * g
  gg_ref[...] = gg
  gbon_ref[...] = gg * bonus

  rs = _iota((2 * CHUNK, CHUNK), 0)
  cs = _iota((2 * CHUNK, CHUNK), 1)
  rr = rs & (CHUNK - 1)
  cum_mask = jnp.logical_and((rr >> lt) == (cs >> lt), jnp.logical_or(cs <= rr, rs >= CHUNK))
  cum_mask = jnp.where(cum_mask, 1.0, 0.0).astype(BF16)
  cums, tots = [], []
  for ci in range(nchunk):
    pieces = _split(w[ci * CHUNK:(ci + 1) * CHUNK], NP_CUM)
    cw = _dot(cum_mask, pieces[0])
    for p in pieces[1:]:
      cw = cw + _dot(cum_mask, p)
    cums.append(cw[:CHUNK])
    tots.append(cw[CHUNK:])
  cum = _cat(cums, 0)
  tot = _cat(tots, 0)
  e_neg = jnp.exp(-cum)
  e_rem = jnp.exp(tot - cum)
  ops_ref[0] = (-kkn * jnp.exp(cum - w)).astype(BF16)
  ops_ref[1] = (r * jnp.exp(cum)).astype(BF16)
  ops_ref[2] = (bv * e_neg).astype(BF16)
  ops_ref[3] = (k_h * e_neg).astype(BF16)
  ops_ref[4] = (bv * e_rem).astype(BF16)
  ops_ref[5] = (k_h * e_rem).astype(BF16)
  ops_ref[6] = v.astype(BF16)
  dt_ref[...] = jnp.exp(tot)


def _front(x, w_in_b, prm, l, *, chain, nseq, tlen, tm=256, sprev=None):
  n = x.shape[0]
  if chain:
    nstep = tlen // tm
    grid = (nseq, nstep)
    rmap = lambda b, c: (b * nstep + c, 0)
    omap = lambda b, c: (0, b * nstep + c, 0)
    ktlen = CHUNK
  else:
    grid = (n // tm, 1)
    rmap = lambda b, c: (b, 0)
    omap = lambda b, c: (0, b, 0)
    ktlen = tlen
  row = lambda width: pl.BlockSpec((tm, width), rmap)
  vec = pl.BlockSpec((None, 1, D_MODEL), lambda b, c: (l, 0, 0))
  in_specs = [row(D_MODEL), pl.BlockSpec((None, D_MODEL, IN_COLS), lambda b, c: (l, 0, 0),
                                         pipeline_mode=pl.Buffered(1))]
  args = [x, w_in_b]
  if not chain:
    in_specs.append(row(SHIFT_WIDTH))
    args.append(sprev)
  in_specs += [pl.BlockSpec((None, 1, SHIFT_WIDTH), lambda b, c: (l, 0, 0)),
               pl.BlockSpec((None, 128, 2 * D_MODEL), lambda b, c: (l, 0, 0)),
               vec, vec,
               pl.BlockSpec((None, 128, D_MODEL), lambda b, c: (l, 0, 0)),
               vec, vec, vec]
  args += [prm["shift_mu"], prm["lora_w"], prm["w0"], prm["a0"], prm["gate_up"], prm["k_k"], prm["k_a"],
           prm["r_k"]]
  out_specs = [row(XY_WIDTH), row(D_MODEL), pl.BlockSpec((N_WKV_OPS, tm, D_MODEL), omap),
               row(D_MODEL), row(D_MODEL), row(D_MODEL)]
  f32 = lambda width: jax.ShapeDtypeStruct((n, width), F32)
  out_shape = [f32(XY_WIDTH), f32(D_MODEL), jax.ShapeDtypeStruct((N_WKV_OPS, n, D_MODEL), BF16),
               f32(D_MODEL), f32(D_MODEL), f32(D_MODEL)]
  scratch = []
  if chain:
    out_specs.append(pl.BlockSpec((1, 8, SHIFT_WIDTH), lambda b, c: (b, 0, 0)))
    out_shape.append(jax.ShapeDtypeStruct((nseq, 8, SHIFT_WIDTH), F32))
    scratch = [pltpu.VMEM((8, SHIFT_WIDTH), F32)]
  else:
    out_specs.append(row(SHIFT_WIDTH))
    out_shape.append(f32(SHIFT_WIDTH))
  return pl.pallas_call(
      functools.partial(_front_kernel, chain, ktlen),
      grid=grid, in_specs=in_specs, out_specs=out_specs, out_shape=out_shape, scratch_shapes=scratch,
      compiler_params=_params(("arbitrary", "arbitrary")),
      name="front_chain" if chain else "front_step",
  )(*args)


def _lru_kernel(chain, *refs):
  if chain:
    (x_ref, y_ref, ga_ref, cw_ref, cb_ref, wa_ref, ba_ref, wx_ref, bx_ref, ap_ref,
     pa_ref, hl_ref, ct_ref, cx_scr, ch_scr) = refs
  else:
    (x_ref, y_ref, ga_ref, prev_ref, h0_ref, cw_ref, cb_ref, wa_ref, ba_ref, wx_ref, bx_ref,
     ap_ref, pa_ref, hl_ref) = refs
  rows = x_ref.shape[0]
  nt = rows // 8
  x = x_ref[...]
  x3 = x.reshape(nt, 8, D_MODEL)
  if chain:
    c = pl.program_id(1)

    @pl.when(c == 0)
    def _():
      cx_scr[...] = jnp.zeros_like(cx_scr)
      ch_scr[...] = jnp.zeros_like(ch_scr)

    prev3 = jnp.concatenate([cx_scr[...].reshape(1, 8, D_MODEL), x3[:nt - 1]], axis=0)
    cx_scr[...] = x_ref[rows - 8:rows, :]
    ct_ref[0] = x_ref[rows - 8:rows, :]
  else:
    prev3 = prev_ref[...].reshape(nt, 8, D_MODEL)

  sub = _iota((nt, 8, D_MODEL), 1)
  u = cb_ref[...] + cw_ref[3:4, :] * x
  for j in (1, 2, 3):
    xs = jnp.where(sub >= j, pltpu.roll(x3, j, 1), pltpu.roll(prev3, j, 1))
    u = u + cw_ref[3 - j:4 - j, :] * xs.reshape(rows, D_MODEL)

  gr, gi = [], []
  for g in range(N_GROUPS):
    ug = u[:, _gsl(g)].astype(BF16)
    gr.append(_dot(ug, wa_ref[g]))
    gi.append(_dot(ug, wx_ref[g]))
  gate_r = jax.nn.sigmoid(jnp.concatenate(gr, axis=1) + ba_ref[...])
  gate_i = jax.nn.sigmoid(jnp.concatenate(gi, axis=1) + bx_ref[...])
  log_a = (-LRU_C * gate_r) * _softplus(ap_ref[...])
  a = jnp.exp(log_a)
  mult = jnp.sqrt(-_expm1(2.0 * log_a))
  if chain:
    first = jnp.logical_and(_iota((rows, D_MODEL), 0) == 0, c == 0)
    mult = jnp.where(first, 1.0, mult)
  xin = u * gate_i * mult

  a3 = a.reshape(nt, 8, D_MODEL)
  b3 = xin.reshape(nt, 8, D_MODEL)
  for s in (1, 2, 4):
    m = sub >= s
    b3 = jnp.where(m, a3 * pltpu.roll(b3, s, 1) + b3, b3)
    a3 = jnp.where(m, a3 * pltpu.roll(a3, s, 1), a3)
  if chain:
    hc = ch_scr[7:8, :]
    hs = []
    for i in range(nt):
      hi = a3[i] * hc + b3[i]
      hs.append(hi)
      hc = hi[7:8, :]
    h = jnp.concatenate(hs, axis=0)
    ch_scr[...] = hs[-1]
    hl_ref[0] = hc
  else:
    h3 = a3 * h0_ref[...] + b3
    hl_ref[...] = h3[:, 7:8, :]
    h = h3.reshape(rows, D_MODEL)

  pa_ref[...] = jax.nn.sigmoid(ga_ref[...]) * (h * _gelu_tanh(y_ref[...]))


def _lru(xy, gate_a, prm, l, *, chain, nseq, tlen, prev=None, h0=None):
  if chain:
    rows = 256
    nstep = tlen // rows
    grid = (nseq, nstep)
    rmap = lambda b, c: (b * nstep + c, 0)
    hl_spec = pl.BlockSpec((1, 1, D_MODEL), lambda b, c: (b, 0, 0))
  else:
    nb = 16
    rows = nb * tlen
    grid = (nseq // nb, 1)
    rmap = lambda b, c: (b, 0)
    hl_spec = pl.BlockSpec((nb, 1, D_MODEL), lambda b, c: (b, 0, 0))
  rmap1 = lambda b, c: (rmap(b, c)[0], 1)
  vec = pl.BlockSpec((None, 1, D_MODEL), lambda b, c: (l, 0, 0))
  gates_w = pl.BlockSpec((None, N_GROUPS, GROUP, GROUP), lambda b, c: (l, 0, 0, 0))
  in_specs = [pl.BlockSpec((rows, D_MODEL), rmap),
              pl.BlockSpec((rows, D_MODEL), rmap1),
              pl.BlockSpec((rows, D_MODEL), rmap)]
  args = [xy, xy, gate_a]
  if not chain:
    in_specs += [pl.BlockSpec((rows, D_MODEL), rmap),
                 pl.BlockSpec((nb, 1, D_MODEL), lambda b, c: (b, 0, 0))]
    args += [prev, h0]
  in_specs += [pl.BlockSpec((None, 4, D_MODEL), lambda b, c: (l, 0, 0)), vec, gates_w, vec, gates_w, vec, vec]
  args += [prm["conv_w"], prm["conv_b"], prm["wa_bd"], prm["lru_ba"], prm["wx_bd"], prm["lru_bx"],
           prm["lru_a_param"]]
  out_specs = [pl.BlockSpec((rows, D_MODEL), rmap), hl_spec]
  out_shape = [jax.ShapeDtypeStruct((nseq * tlen, D_MODEL), F32), jax.ShapeDtypeStruct((nseq, 1, D_MODEL), F32)]
  scratch = []
  if chain:
    out_specs.append(pl.BlockSpec((1, 8, D_MODEL), lambda b, c: (b, 0, 0)))
    out_shape.append(jax.ShapeDtypeStruct((nseq, 8, D_MODEL), F32))
    scratch = [pltpu.VMEM((8, D_MODEL), F32), pltpu.VMEM((8, D_MODEL), F32)]
  return pl.pallas_call(
      functools.partial(_lru_kernel, chain),
      grid=grid, in_specs=in_specs, out_specs=out_specs, out_shape=out_shape, scratch_shapes=scratch,
      compiler_params=_params(("arbitrary", "arbitrary")),
      name="lru_chain" if chain else "lru_step",
  )(*args)


def _wkv_masks(tlen):
  lt = tlen.bit_length() - 1
  ri = _iota((CHUNK, GROUP), 0)
  cm = _iota((CHUNK, GROUP), 1) & (CHUNK - 1)
  same = (ri >> lt) == (cm >> lt)
  lane = _iota((CHUNK, LANES), 1)
  half = [jnp.where(lane < HEAD, 1.0, 0.0), jnp.where(lane >= HEAD, 1.0, 0.0)]
  return dict(
      strict=jnp.logical_and(same, cm < ri),
      incl=jnp.logical_and(same, cm <= ri),
      eye=jnp.where(cm == ri, 1.0, 0.0),
      half=half,
      half_b=[h.astype(BF16) for h in half],
      zero_b=jnp.zeros((CHUNK, LANES), BF16),
      ones_bd=_ones_bd(),
      colseq=(_iota((GROUP, 2 * CHUNK), 1) & (CHUNK - 1)) >> lt,
      lt=lt,
  )


def _bd_from_tiles(tiles, mk):
  z = mk["zero_b"]
  return jnp.concatenate([
      jnp.concatenate([tiles[0], z], axis=1), jnp.concatenate([tiles[1], z], axis=1),
      jnp.concatenate([z, tiles[2]], axis=1), jnp.concatenate([z, tiles[3]], axis=1)], axis=0)


def _bd(x, mk):
  xb = x.astype(BF16)
  return _bd_from_tiles([xb[:, (i // 2) * LANES:(i // 2 + 1) * LANES] * mk["half_b"][i % 2] for i in range(4)],
                        mk)


def _wkv_kernel(chain, tlen, cps, *refs):
  if chain:
    ops_ref, dt_ref, pa_ref, gg_ref, gbon_ref, gnw_ref, gnb_ref, out_ref, sout_ref, s_scr = refs
  else:
    ops_ref, dt_ref, pa_ref, gg_ref, gbon_ref, s0_ref, gnw_ref, gnb_ref, out_ref, sout_ref = refs
  nsq = CHUNK // tlen
  mk = _wkv_masks(tlen)
  groups = range(N_GROUPS)
  units = [(ci, gi) for ci in range(cps) for gi in groups]
  cut = lambda ref, un: ref[un[0] * CHUNK:(un[0] + 1) * CHUNK, _gsl(un[1])]
  op = lambda idx, un: ops_ref[idx, un[0] * CHUNK:(un[0] + 1) * CHUNK, _gsl(un[1])]
  tile_of = lambda i: slice((i // 2) * LANES, (i // 2 + 1) * LANES)
  rblk = lambda i: slice(i * HEAD, (i + 1) * HEAD)

  if chain:
    c = pl.program_id(1)

    @pl.when(c == 0)
    def _():
      s_scr[...] = jnp.zeros_like(s_scr)

  lhs, n_ab, a_ak, a_rbk = {}, {}, {}, {}
  for un in units:
    lhs[un] = jnp.concatenate([op(0, un), op(1, un)], axis=0)
    gb = _dot(lhs[un], _bd(op(2, un), mk), trans_b=True)
    gk = _dot(lhs[un], _bd(op(3, un), mk), trans_b=True)
    n_ab[un] = jnp.where(mk["strict"], gb[:CHUNK], 0.0)
    a_ak[un] = jnp.where(mk["strict"], gk[:CHUNK], 0.0)
    a_rbk[un] = jnp.concatenate([jnp.where(mk["incl"], gb[CHUNK:], 0.0),
                                 jnp.where(mk["incl"], gk[CHUNK:], 0.0)], axis=1)

  t = {un: mk["eye"] + n_ab[un] for un in units}
  q = {un: _dot(n_ab[un], _bd(n_ab[un], mk)) for un in units}
  levels = mk["lt"]
  for lvl in range(1, levels):
    for un in units:
      bq = _bd(q[un], mk)
      if lvl + 1 < levels:
        tq = _dot(jnp.concatenate([t[un], q[un]], axis=0), bq)
        t[un] = t[un] + tq[:CHUNK]
        q[un] = tq[CHUNK:]
      else:
        t[un] = t[un] + _dot(t[un], bq)

  bdv, a_v, xt, dcol = {}, {}, {}, {}
  for un in units:
    bdv[un] = _bd(op(6, un), mk)
    a_v[un] = _dot(a_ak[un], bdv[un])
    dt = cut(dt_ref, un)
    xpt = jnp.concatenate([op(4, un).astype(F32), op(5, un).astype(F32), dt, dt], axis=0).T
    xt[un] = xpt[:, :2 * CHUNK]
    dcol[un] = xpt[:, 2 * CHUNK:]

  if chain:
    p_cur = {gi: [s_scr[gi, i] for i in range(4)] for gi in groups}
  o = {}
  for ci in range(cps):
    states, a_p, r_p, u = {}, {}, {}, {}
    for gi in groups:
      un = (ci, gi)
      if chain:
        states[gi] = [p_cur[gi]]
      else:
        states[gi] = [[s0_ref[ci * nsq + j, gi][:, tile_of(i)] * mk["half"][i % 2] for i in range(4)]
                      for j in range(nsq)]
      sbd = [_bd_from_tiles([tl.astype(BF16) for tl in st], mk) for st in states[gi]]
      if nsq == 1:
        lp = _dot(lhs[un], sbd[0])
        a_p[gi], r_p[gi] = lp[:CHUNK], lp[CHUNK:]
      else:
        aps, rps = [], []
        for j in range(nsq):
          lj = jnp.concatenate([lhs[un][j * tlen:(j + 1) * tlen],
                                lhs[un][CHUNK + j * tlen:CHUNK + (j + 1) * tlen]], axis=0)
          lpj = _dot(lj, sbd[j])
          aps.append(lpj[:tlen])
          rps.append(lpj[tlen:])
        a_p[gi] = jnp.concatenate(aps, axis=0)
        r_p[gi] = jnp.concatenate(rps, axis=0)
    for gi in groups:
      un = (ci, gi)
      u[gi] = _dot(t[un], _bd(a_p[gi] + a_v[un], mk))
    for gi in groups:
      un = (ci, gi)
      y = jnp.concatenate([u[gi].astype(BF16), op(6, un)], axis=0)
      for j in range(nsq):
        xtj = xt[un] if nsq == 1 else jnp.where(mk["colseq"] == j, xt[un], 0.0)
        upd = _dot(xtj, y)
        new = []
        for i in range(4):
          if nsq == 1:
            dcb = dcol[un][rblk(i)]
          else:
            dcb = jnp.broadcast_to(dcol[un][rblk(i), j * tlen:j * tlen + 1], (HEAD, LANES))
          new.append((states[gi][j][i] * dcb + upd[rblk(i), tile_of(i)]) * mk["half"][i % 2])
        if chain:
          p_cur[gi] = new
        else:
          sout_ref[ci * nsq + j, gi] = jnp.concatenate([new[0] + new[1], new[2] + new[3]], axis=1)
    for gi in groups:
      un = (ci, gi)
      o[un] = r_p[gi] + _dot(a_rbk[un], jnp.concatenate([_bd(u[gi], mk), bdv[un]], axis=0))
  if chain:
    for gi in groups:
      for i in range(4):
        s_scr[gi, i] = p_cur[gi][i]

    @pl.when(c == pl.num_programs(1) - 1)
    def _():
      for gi in groups:
        sout_ref[0, gi] = jnp.concatenate([s_scr[gi, 0] + s_scr[gi, 1], s_scr[gi, 2] + s_scr[gi, 3]], axis=1)

  outs = []
  for gi in groups:
    og = _cat([o[(ci, gi)] for ci in range(cps)], 0)
    o_mu = _dot(og, mk["ones_bd"]) * (1.0 / HEAD)
    oc = og - o_mu
    o_var = _dot(oc * oc, mk["ones_bd"]) * (1.0 / HEAD)
    outs.append(oc * lax.rsqrt(o_var + GN_EPS) * gnw_ref[:, _gsl(gi)] + gnb_ref[:, _gsl(gi)])
  out_ref[...] = pa_ref[...] + gbon_ref[...] + gg_ref[...] * jnp.concatenate(outs, axis=1)


def _wkv(ops, dtot, pa, gg, gbon, prm, l, *, chain, nseq, tlen, cps, s0=None, s0_blk0=0):
  rows = cps * CHUNK
  if chain:
    nstep = tlen // rows
    grid = (nseq, nstep)
    rmap = lambda b, c: (b * nstep + c, 0)
    omap = lambda b, c: (0, b * nstep + c, 0)
    sblk = 1
    ktlen = CHUNK
  else:
    sblk = rows // tlen
    grid = (nseq // sblk, 1)
    rmap = lambda b, c: (b, 0)
    omap = lambda b, c: (0, b, 0)
    ktlen = tlen
  smap = lambda b, c: (b, 0, 0, 0)
  row = pl.BlockSpec((rows, D_MODEL), rmap)
  vec = pl.BlockSpec((None, 1, D_MODEL), lambda b, c: (l, 0, 0))
  in_specs = [pl.BlockSpec((N_WKV_OPS, rows, D_MODEL), omap), row, row, row, row]
  args = [ops, dtot, pa, gg, gbon]
  if not chain:
    in_specs.append(pl.BlockSpec((sblk, N_GROUPS, HEAD, GROUP), lambda b, c: (s0_blk0 + b, 0, 0, 0)))
    args.append(s0)
  in_specs += [vec, vec]
  args += [prm["gn_w"], prm["gn_b"]]
  scratch = [pltpu.VMEM((N_GROUPS, 4, HEAD, LANES), F32)] if chain else []
  return pl.pallas_call(
      functools.partial(_wkv_kernel, chain, ktlen, cps),
      grid=grid, in_specs=in_specs,
      out_specs=[row, pl.BlockSpec((sblk, N_GROUPS, HEAD, GROUP), smap)],
      out_shape=[jax.ShapeDtypeStruct((nseq * tlen, D_MODEL), F32),
                 jax.ShapeDtypeStruct((nseq, N_GROUPS, HEAD, GROUP), F32)],
      scratch_shapes=scratch,
      compiler_params=_params(("arbitrary", "arbitrary")),
      name="wkv_chain" if chain else "wkv_step",
  )(*args)


def _mlp_kernel(x_ref, m_ref, wo_ref, g1_ref, b1_ref, w1_ref, w2_ref, g2_ref, b2_ref,
                o_ref, x1_scr, x1b_scr, acc_scr):
  j = pl.program_id(1)

  @pl.when(j == 0)
  def _():
    y = DN_ALPHA * x_ref[...] + _dot(m_ref[...], wo_ref[...])
    x1 = _layer_norm(y, g1_ref[...], b1_ref[...])
    x1_scr[...] = x1
    x1b_scr[...] = x1.astype(BF16)
    acc_scr[...] = jnp.zeros_like(acc_scr)

  hid = jnp.square(jnp.maximum(_dot(x1b_scr[...], w1_ref[...]), 0.0))
  acc_scr[...] += _dot(hid, w2_ref[...])

  @pl.when(j == pl.num_programs(1) - 1)
  def _():
    o_ref[...] = _layer_norm(DN_ALPHA * x1_scr[...] + acc_scr[...], g2_ref[...], b2_ref[...])


def _mlp(x, merged, prm, l, tm=512, tf=1024):
  n = x.shape[0]
  vec = pl.BlockSpec((None, 1, D_MODEL), lambda i, j: (l, 0, 0))
  row = pl.BlockSpec((tm, D_MODEL), lambda i, j: (i, 0))
  return pl.pallas_call(
      _mlp_kernel,
      grid=(n // tm, D_FF // tf),
      in_specs=[
          row, row,
          pl.BlockSpec((None, D_MODEL, D_MODEL), lambda i, j: (l, 0, 0)),
          vec, vec,
          pl.BlockSpec((None, D_MODEL, tf), lambda i, j: (l, 0, j)),
          pl.BlockSpec((None, tf, D_MODEL), lambda i, j: (l, j, 0)),
          vec, vec,
      ],
      out_specs=row,
      out_shape=jax.ShapeDtypeStruct((n, D_MODEL), F32),
      scratch_shapes=[pltpu.VMEM((tm, D_MODEL), F32), pltpu.VMEM((tm, D_MODEL), BF16),
                      pltpu.VMEM((tm, D_MODEL), F32)],
      compiler_params=_params(("arbitrary", "arbitrary")),
      name="outproj_mlp",
  )(x, merged, prm["w_out"], prm["ln1_g"], prm["ln1_b"], prm["mlp_w1"], prm["mlp_w2"],
    prm["ln2_g"], prm["ln2_b"])


def _block_diag4(w):
  depth = w.shape[0]
  w5 = w.reshape(depth, N_GROUPS, 4, HEAD, HEAD)
  eye = jnp.eye(4, dtype=w.dtype)
  return jnp.einsum("lgaij,ab->lgaibj", w5, eye).reshape(depth, N_GROUPS, GROUP, GROUP)


def _state_to_cat(s):
  b = s.shape[0]
  return s.reshape(b, N_GROUPS, 4, HEAD, HEAD).transpose(0, 1, 4, 2, 3).reshape(b, N_GROUPS, HEAD, GROUP)


def _cat_to_state(z):
  b = z.shape[0]
  return z.reshape(b, N_GROUPS, HEAD, 4, HEAD).transpose(0, 1, 3, 4, 2).reshape(b, 4 * N_GROUPS, HEAD, HEAD)


def kernel(x_prompt, x_sample, state_conv, state_lru, state_shift, state_wkv, w_in, conv_w, conv_b, lru_wa,
           lru_ba, lru_wx, lru_bx, lru_a_param, shift_mu, decay_up, w0, aaa_up, a0, gate_up, k_k, k_a, r_k,
           gn_w, gn_b, w_out, ln1_g, ln1_b, mlp_w1, mlp_w2, ln2_g, ln2_b):
  bp, tp, _ = x_prompt.shape
  bs, ts, _ = x_sample.shape
  depth = w_in.shape[0]

  row = lambda p: p.reshape(depth, 1, -1)
  w_in_b = w_in.astype(BF16)
  zero = jnp.zeros((depth, 64, D_MODEL), F32)
  lora_w = jnp.concatenate([jnp.concatenate([decay_up, zero], axis=2),
                            jnp.concatenate([zero, aaa_up], axis=2)], axis=1).astype(BF16)
  prm = dict(
      conv_w=conv_w, conv_b=row(conv_b),
      wa_bd=_block_diag4(lru_wa).astype(BF16), wx_bd=_block_diag4(lru_wx).astype(BF16),
      lru_ba=row(lru_ba), lru_bx=row(lru_bx), lru_a_param=row(lru_a_param),
      shift_mu=row(shift_mu), lora_w=lora_w, w0=row(w0), a0=row(a0), gate_up=gate_up.astype(BF16),
      k_k=row(k_k), k_a=row(k_a), r_k=row(r_k), gn_w=row(gn_w), gn_b=row(gn_b),
      w_out=w_out.astype(BF16), ln1_g=row(ln1_g), ln1_b=row(ln1_b),
      mlp_w1=mlp_w1.astype(BF16), mlp_w2=mlp_w2.astype(BF16), ln2_g=row(ln2_g), ln2_b=row(ln2_b),
  )
  s0_all = _state_to_cat(state_wkv.reshape(depth * bs, 4 * N_GROUPS, HEAD, HEAD))
  cps_s = 2
  seq_per_step = cps_s * CHUNK // ts

  xp = x_prompt.reshape(bp * tp, D_MODEL)
  xs = x_sample.reshape(bs * ts, D_MODEL)
  conv_p, lru_p, shift_p, wkv_p = [], [], [], []
  conv_s, lru_s, shift_s, wkv_s = [], [], [], []
  for l in range(depth):
    xy, ga, ops, dtot, gg, gbon, stail = _front(xp, w_in_b, prm, l, chain=True, nseq=bp, tlen=tp)
    pa, hl, ctail = _lru(xy, ga, prm, l, chain=True, nseq=bp, tlen=tp)
    mg, z = _wkv(ops, dtot, pa, gg, gbon, prm, l, chain=True, nseq=bp, tlen=tp, cps=2)
    xp = _mlp(xp, mg, prm, l)
    conv_p.append(ctail[:, 5:, :])
    lru_p.append(hl.reshape(bp, D_MODEL))
    shift_p.append(stail[:, 7, :])
    wkv_p.append(z)

    xy, ga, ops, dtot, gg, gbon, sh = _front(xs, w_in_b, prm, l, chain=False, nseq=bs, tlen=ts, tm=128,
                                             sprev=jnp.repeat(state_shift[l], ts, axis=0))
    prev = jnp.pad(state_conv[l], ((0, 0), (5, 0), (0, 0))).reshape(bs * 8, D_MODEL)
    pa, hl = _lru(xy, ga, prm, l, chain=False, nseq=bs, tlen=ts, prev=prev,
                  h0=state_lru[l].reshape(bs, 1, D_MODEL))
    mg, z = _wkv(ops, dtot, pa, gg, gbon, prm, l, chain=False, nseq=bs, tlen=ts, cps=cps_s,
                 s0=s0_all, s0_blk0=l * (bs // seq_per_step))
    xs = _mlp(xs, mg, prm, l)
    conv_s.append(xy.reshape(bs, ts, XY_WIDTH)[:, ts - 3:, :D_MODEL])
    lru_s.append(hl.reshape(bs, D_MODEL))
    shift_s.append(sh.reshape(bs, ts, SHIFT_WIDTH)[:, ts - 1])
    wkv_s.append(z)

  unstate = lambda zs, b: _cat_to_state(jnp.concatenate(zs, axis=0)).reshape(depth, b, 4 * N_GROUPS, HEAD, HEAD)
  return (xp.reshape(bp, tp, D_MODEL), xs.reshape(bs, ts, D_MODEL),
          jnp.stack(conv_p), jnp.stack(lru_p), jnp.stack(shift_p), unstate(wkv_p, bp),
          jnp.stack(conv_s), jnp.stack(lru_s), jnp.stack(shift_s), unstate(wkv_s, bs))
```

```python
import functools
import math

import jax
import jax.numpy as jnp
from jax import lax
from jax.experimental import pallas as pl
from jax.experimental.pallas import tpu as pltpu

F32 = jnp.float32
BF16 = jnp.bfloat16

D_MODEL = 1024
DEPTH = 4
LRU_C = 8.0
HEAD = 64
GROUP = 256
LANES = 128
N_GROUPS = D_MODEL // GROUP
SHIFT_WIDTH = 3 * D_MODEL + 64 + 64 + 128
XY_WIDTH = 2 * D_MODEL
IN_COLS = 2 * XY_WIDTH + SHIFT_WIDTH
D_FF = 4 * D_MODEL
DN_ALPHA = (2 * DEPTH) ** 0.25
LN_EPS = 1e-5
GN_EPS = 64e-5
CHUNK = 64
VMEM_LIMIT = 56 * 1024 * 1024
NP_CUM = 2
N_WKV_OPS = 7


def _split(x, n):
  pieces = []
  rem = x
  for i in range(n):
    p = rem.astype(BF16)
    pieces.append(p)
    if i + 1 < n:
      rem = rem - p.astype(F32)
  return pieces


def _dot(a, b, trans_b=False):
  dims = (((1,), (1,)), ((), ())) if trans_b else (((1,), (0,)), ((), ()))
  return lax.dot_general(a.astype(BF16), b.astype(BF16), dims, preferred_element_type=F32)


def _softplus(x):
  return jnp.maximum(x, 0.0) + jnp.log1p(jnp.exp(-jnp.abs(x)))


def _expm1(x):
  u = jnp.exp(x)
  lu = jnp.log(u)
  small = jnp.where(u == 1.0, x, (u - 1.0) * x / jnp.where(u == 1.0, 1.0, lu))
  return jnp.where(jnp.abs(x) < 0.5, small, u - 1.0)


def _gelu_tanh(x):
  c = math.sqrt(2.0 / math.pi)
  return x * (0.5 * (1.0 + jnp.tanh(c * (x + 0.044715 * (x * x * x)))))


def _layer_norm(x, g, b):
  mu = jnp.mean(x, axis=-1, keepdims=True)
  xc = x - mu
  var = jnp.mean(xc * xc, axis=-1, keepdims=True)
  return xc * lax.rsqrt(var + LN_EPS) * g + b


def _iota(shape, dim):
  return lax.broadcasted_iota(jnp.int32, shape, dim)


def _cat(xs, axis):
  return xs[0] if len(xs) == 1 else jnp.concatenate(xs, axis=axis)


def _params(sem):
  return pltpu.CompilerParams(dimension_semantics=sem, vmem_limit_bytes=VMEM_LIMIT)


def _ones_bd():
  same = (_iota((GROUP, GROUP), 0) >> 6) == (_iota((GROUP, GROUP), 1) >> 6)
  return jnp.where(same, 1.0, 0.0).astype(BF16)


def _gsl(gi):
  return slice(gi * GROUP, (gi + 1) * GROUP)


def _front_dots(x_ref, w_ref, xy_ref, ga_ref):
  xb = x_ref[...].astype(BF16)
  sh = _dot(xb, w_ref[:, XY_WIDTH:XY_WIDTH + SHIFT_WIDTH])
  xy_ref[...] = _dot(xb, w_ref[:, 0:XY_WIDTH])
  gt = _dot(xb, w_ref[:, XY_WIDTH + SHIFT_WIDTH:IN_COLS])
  ga_ref[...] = gt[:, :D_MODEL]
  return sh, gt[:, D_MODEL:]


def _front_prep(sh, gate_b, prev, tlen, prm_refs, out_refs, r0=0):
  mu_ref, lw_ref, w0_ref, a0_ref, gup_ref, kk_ref, ka_ref, rk_ref = prm_refs
  ops_ref, dt_ref, gg_ref, gbon_ref = out_refs
  rows = sh.shape[0]
  rsl = slice(r0, r0 + rows)
  nchunk = rows // CHUNK
  lt = tlen.bit_length() - 1
  ones_bd = _ones_bd()
  headsum = lambda z: _cat([_dot(z[:, _gsl(gi)], ones_bd) for gi in range(N_GROUPS)], 1)

  mixed = sh + (prev - sh) * mu_ref[...]
  r = mixed[:, 0:D_MODEL]
  k = mixed[:, D_MODEL:2 * D_MODEL]
  v = mixed[:, 2 * D_MODEL:3 * D_MODEL]
  xl = mixed[:, 3 * D_MODEL:3 * D_MODEL + 128]
  xg = mixed[:, 3 * D_MODEL + 128:SHIFT_WIDTH]
  lane = _iota((rows, 128), 1)
  lo = _dot(jnp.where(lane < 64, jnp.tanh(xl), xl), lw_ref[...])
  g = _dot(jax.nn.sigmoid(xg), gup_ref[...])
  kk = k * kk_ref[...]
  ss = headsum(kk * kk)
  yield

  w_log = -_softplus(-(w0_ref[...] + lo[:, :D_MODEL])) - 0.5
  w = -jnp.exp(w_log)
  a = jax.nn.sigmoid(a0_ref[...] + lo[:, D_MODEL:])
  k_h = k * (1.0 + (a - 1.0) * ka_ref[...])
  rkr = r * k_h * rk_ref[...]
  bonus_sum = headsum(rkr)
  rs = _iota((2 * CHUNK, CHUNK), 0)
  cs = _iota((2 * CHUNK, CHUNK), 1)
  rr = rs & (CHUNK - 1)
  cum_mask = jnp.logical_and((rr >> lt) == (cs >> lt), jnp.logical_or(cs <= rr, rs >= CHUNK))
  cum_mask = jnp.where(cum_mask, 1.0, 0.0).astype(BF16)
  one_seq = tlen == CHUNK
  if one_seq:
    cum_mask = cum_mask[:CHUNK]
  cums, tots = [], []
  for ci in range(nchunk):
    pieces = _split(w[ci * CHUNK:(ci + 1) * CHUNK], NP_CUM)
    cw = _dot(cum_mask, pieces[0])
    for p in pieces[1:]:
      cw = cw + _dot(cum_mask, p)
    cums.append(cw[:CHUNK])
    tots.append(jnp.broadcast_to(cw[CHUNK - 1:CHUNK], (CHUNK, D_MODEL)) if one_seq else cw[CHUNK:])
  yield

  kkn = kk / jnp.maximum(jnp.sqrt(ss), 1e-12)
  bv = kkn * a
  gg = jax.nn.sigmoid(gate_b) * g
  gg_ref[rsl, :] = gg
  gbon_ref[rsl, :] = gg * (bonus_sum * v)
  yield
  cum = _cat(cums, 0)
  tot = _cat(tots, 0)
  e_neg = jnp.exp(-cum)
  e_rem = jnp.exp(tot - cum)
  ops_ref[0, rsl, :] = (-kkn * jnp.exp(cum - w)).astype(BF16)
  ops_ref[1, rsl, :] = (r * jnp.exp(cum)).astype(BF16)
  ops_ref[2, rsl, :] = (bv * e_neg).astype(BF16)
  ops_ref[3, rsl, :] = (k_h * e_neg).astype(BF16)
  yield
  ops_ref[4, rsl, :] = (bv * e_rem).astype(BF16)
  ops_ref[5, rsl, :] = (k_h * e_rem).astype(BF16)
  ops_ref[6, rsl, :] = v.astype(BF16)
  dt_ref[rsl, :] = jnp.exp(tot)


def _front_prep_fine(mix, gate_b, prm_refs, out_refs, r0):
  mu_ref, lw_ref, w0_ref, a0_ref, gup_ref, kk_ref, ka_ref, rk_ref = prm_refs
  ops_ref, dt_ref, gg_ref, gbon_ref = out_refs
  rsl = slice(r0, r0 + CHUNK)
  ones_bd = _ones_bd()
  cum_mask = jnp.where(_iota((CHUNK, CHUNK), 1) <= _iota((CHUNK, CHUNK), 0), 1.0, 0.0).astype(BF16)
  lane = _iota((CHUNK, LANES), 1)
  col = lambda base, gi: slice(base + gi * GROUP, base + (gi + 1) * GROUP)

  xl = mix(slice(3 * D_MODEL, 3 * D_MODEL + LANES))
  xg = mix(slice(3 * D_MODEL + LANES, SHIFT_WIDTH))
  lo = _dot(jnp.where(lane < 64, jnp.tanh(xl), xl), lw_ref[...])
  g = _dot(jax.nn.sigmoid(xg), gup_ref[...])
  yield
  st = {}
  for gi in range(N_GROUPS):
    k = mix(col(D_MODEL, gi))
    kk = k * kk_ref[:, _gsl(gi)]
    st[gi] = dict(k=k, kk=kk, ss=_dot(kk * kk, ones_bd))
    yield
  for gi in range(N_GROUPS):
    cs, d = _gsl(gi), st[gi]
    w = -jnp.exp(-_softplus(-(w0_ref[:, cs] + lo[:, cs])) - 0.5)
    a = jax.nn.sigmoid(a0_ref[:, cs] + lo[:, col(D_MODEL, gi)])
    k_h = d["k"] * (1.0 + (a - 1.0) * ka_ref[:, cs])
    r = mix(col(0, gi))
    bsum = _dot(r * k_h * rk_ref[:, cs], ones_bd)
    pieces = _split(w, NP_CUM)
    cum = _dot(cum_mask, pieces[0])
    for p in pieces[1:]:
      cum = cum + _dot(cum_mask, p)
    d.update(w=w, a=a, k_h=k_h, r=r, bsum=bsum, cum=cum)
    yield
  for gi in range(N_GROUPS):
    cs, d = _gsl(gi), st[gi]
    v = mix(col(2 * D_MODEL, gi))
    kkn = d["kk"] / jnp.maximum(jnp.sqrt(d["ss"]), 1e-12)
    bv = kkn * d["a"]
    gg = jax.nn.sigmoid(gate_b(cs)) * g[:, cs]
    gg_ref[rsl, cs] = gg
    gbon_ref[rsl, cs] = gg * (d["bsum"] * v)
    cum = d["cum"]
    tot = jnp.broadcast_to(cum[CHUNK - 1:CHUNK], cum.shape)
    e_neg = jnp.exp(-cum)
    e_rem = jnp.exp(tot - cum)
    ops_ref[0, rsl, cs] = (-kkn * jnp.exp(cum - d["w"])).astype(BF16)
    ops_ref[1, rsl, cs] = (d["r"] * jnp.exp(cum)).astype(BF16)
    ops_ref[2, rsl, cs] = (bv * e_neg).astype(BF16)
    ops_ref[3, rsl, cs] = (d["k_h"] * e_neg).astype(BF16)
    ops_ref[4, rsl, cs] = (bv * e_rem).astype(BF16)
    ops_ref[5, rsl, cs] = (d["k_h"] * e_rem).astype(BF16)
    ops_ref[6, rsl, cs] = v.astype(BF16)
    dt_ref[rsl, cs] = jnp.exp(tot)
    yield


def _front_step_kernel(tlen, x_ref, w_ref, sp_ref, *refs):
  prm_refs, (xy_ref, ga_ref), out_refs, sh_ref = refs[:8], refs[8:10], refs[10:14], refs[14]
  sh, gate_b = _front_dots(x_ref, w_ref, xy_ref, ga_ref)
  sh_ref[...] = sh
  row = _iota(sh.shape, 0)
  prev = jnp.where((row & (tlen - 1)) == 0, sp_ref[...], pltpu.roll(sh, 1, 0))
  for _ in _front_prep(sh, gate_b, prev, tlen, prm_refs, out_refs):
    pass


def _front_chain_kernel(nstep, ntiles, x_ref, w_ref, *refs):
  prm_refs, (xy_ref, ga_ref), out_refs = refs[:8], refs[8:10], refs[10:14]
  st_ref, sh_a, sh_b, gb_a, gb_b, carry_scr = refs[14:]
  s = pl.program_id(0)
  rows = x_ref.shape[0]
  bufs = ((sh_a, gb_a), (sh_b, gb_b))

  nchunk = rows // CHUNK
  step_w = GROUP
  pieces = [("xy", c, c + step_w) for c in range(0, XY_WIDTH, step_w)]
  pieces += [("sh", c, min(c + step_w, SHIFT_WIDTH)) for c in range(0, SHIFT_WIDTH, step_w)]
  pieces += [("ga", c, c + step_w) for c in range(0, D_MODEL, step_w)]
  pieces += [("gb", c, c + step_w) for c in range(0, D_MODEL, step_w)]
  w_base = {"xy": 0, "sh": XY_WIDTH, "ga": XY_WIDTH + SHIFT_WIDTH, "gb": XY_WIDTH + SHIFT_WIDTH + D_MODEL}

  def dot_piece(parity, piece):
    sh_scr, gb_scr = bufs[parity]
    dst, c0, c1 = piece
    res = _dot(x_ref[...].astype(BF16), w_ref[:, w_base[dst] + c0:w_base[dst] + c1])
    {"xy": xy_ref, "sh": sh_scr, "ga": ga_ref, "gb": gb_scr}[dst][:, c0:c1] = res

  def prep_piece(parity, ci):
    sh_scr, gb_scr = bufs[parity]
    r0 = ci * CHUNK
    row0 = _iota((CHUNK, LANES), 0) == 0
    seq_start = ((s - 1) % nstep) == 0

    def mix(cs):
      sh = sh_scr[r0:r0 + CHUNK, cs]
      first = jnp.where(seq_start, 0.0, carry_scr[7:8, cs]) if ci == 0 else sh_scr[r0 - 1:r0, cs]
      width = sh.shape[1]
      prev = jnp.where(jnp.concatenate([row0] * (width // LANES), axis=1), first, pltpu.roll(sh, 1, 0))
      return sh + (prev - sh) * mu_ref[:, cs]

    return _front_prep_fine(mix, lambda cs: gb_scr[r0:r0 + CHUNK, cs], prm_refs, out_refs, r0)

  def prep_all(parity, between=lambda done, total: None):
    total = nchunk * (1 + 3 * N_GROUPS)
    done = 0
    for ci in range(nchunk):
      for _ in prep_piece(parity, ci):
        done += 1
        between(done, total)
    tail = bufs[parity][0][rows - 8:rows, :]
    carry_scr[...] = tail
    st_ref[0] = tail

  mu_ref = prm_refs[0]

  @pl.when(s == 0)
  def _():
    for piece in pieces:
      dot_piece(0, piece)

  for parity in (0, 1):
    @pl.when(jnp.logical_and(jnp.logical_and(s > 0, s < ntiles), s % 2 == parity))
    def _(parity=parity):
      todo = list(pieces)

      def between(done, total):
        while todo and (len(pieces) - len(todo)) * total < done * len(pieces):
          dot_piece(parity, todo.pop(0))

      prep_all(1 - parity, between)
      for piece in todo:
        dot_piece(parity, piece)

  @pl.when(s == ntiles)
  def _():
    prep_all((ntiles - 1) % 2)


def _front(x, w_in_b, prm, l, *, chain, nseq, tlen, tm=256, sprev=None):
  n = x.shape[0]
  ntiles = n // tm
  if chain:
    nstep = tlen // tm
    grid = (ntiles + 1,)
    cur = lambda s: jnp.minimum(s, ntiles - 1)
    old = lambda s: jnp.maximum(s - 1, 0)
    body = functools.partial(_front_chain_kernel, nstep, ntiles)
  else:
    grid = (ntiles,)
    cur = old = lambda s: s
    body = functools.partial(_front_step_kernel, tlen)
  row = lambda width, which: pl.BlockSpec((tm, width), lambda s: (which(s), 0))
  vec = pl.BlockSpec((None, 1, D_MODEL), lambda s: (l, 0, 0))
  in_specs = [row(D_MODEL, cur), pl.BlockSpec((None, D_MODEL, IN_COLS), lambda s: (l, 0, 0),
                                              pipeline_mode=pl.Buffered(1))]
  args = [x, w_in_b]
  if not chain:
    in_specs.append(row(SHIFT_WIDTH, cur))
    args.append(sprev)
  in_specs += [pl.BlockSpec((None, 1, SHIFT_WIDTH), lambda s: (l, 0, 0)),
               pl.BlockSpec((None, 128, 2 * D_MODEL), lambda s: (l, 0, 0)),
               vec, vec,
               pl.BlockSpec((None, 128, D_MODEL), lambda s: (l, 0, 0)),
               vec, vec, vec]
  args += [prm["shift_mu"], prm["lora_w"], prm["w0"], prm["a0"], prm["gate_up"], prm["k_k"], prm["k_a"],
           prm["r_k"]]
  out_specs = [row(XY_WIDTH, cur), row(D_MODEL, cur),
               pl.BlockSpec((N_WKV_OPS, tm, D_MODEL), lambda s: (0, old(s), 0)),
               row(D_MODEL, old), row(D_MODEL, old), row(D_MODEL, old)]
  f32 = lambda width: jax.ShapeDtypeStruct((n, width), F32)
  out_shape = [f32(XY_WIDTH), f32(D_MODEL), jax.ShapeDtypeStruct((N_WKV_OPS, n, D_MODEL), BF16),
               f32(D_MODEL), f32(D_MODEL), f32(D_MODEL)]
  scratch = []
  if chain:
    out_specs.append(pl.BlockSpec((1, 8, SHIFT_WIDTH), lambda s: (old(s) // nstep, 0, 0)))
    out_shape.append(jax.ShapeDtypeStruct((nseq, 8, SHIFT_WIDTH), F32))
    scratch = [pltpu.VMEM((tm, SHIFT_WIDTH), F32), pltpu.VMEM((tm, SHIFT_WIDTH), F32),
               pltpu.VMEM((tm, D_MODEL), F32), pltpu.VMEM((tm, D_MODEL), F32),
               pltpu.VMEM((8, SHIFT_WIDTH), F32)]
  else:
    out_specs.append(row(SHIFT_WIDTH, cur))
    out_shape.append(f32(SHIFT_WIDTH))
  return pl.pallas_call(
      body, grid=grid, in_specs=in_specs, out_specs=out_specs, out_shape=out_shape, scratch_shapes=scratch,
      compiler_params=_params(("arbitrary",)),
      name="front_chain" if chain else "front_step",
  )(*args)


class _Staged:
  def __init__(self, gen):
    self.gen, self.value, self.done = gen, None, False

  def tick(self):
    if not self.done:
      try:
        next(self.gen)
      except StopIteration as stop:
        self.value, self.done = stop.value, True

  def finish(self):
    while not self.done:
      self.tick()
    return self.value


def _lru_step_kernel(*refs):
  n_in = 12
  refs[n_in][...] = _Staged(_lru_body(False, *refs[:n_in], *refs[n_in + 1:])).finish()


def _lru_body(chain, *refs):
  if chain:
    (x_ref, y_ref, ga_ref, cw_ref, cb_ref, wa_ref, ba_ref, wx_ref, bx_ref, ap_ref,
     hl_ref, ct_ref, cx_scr, ch_scr) = refs
  else:
    (x_ref, y_ref, ga_ref, prev_ref, h0_ref, cw_ref, cb_ref, wa_ref, ba_ref, wx_ref, bx_ref,
     ap_ref, hl_ref) = refs
  rows = x_ref.shape[0]
  nt = rows // 8
  x = x_ref[...]
  x3 = x.reshape(nt, 8, D_MODEL)
  if chain:
    c = pl.program_id(1)

    @pl.when(c == 0)
    def _():
      cx_scr[...] = jnp.zeros_like(cx_scr)
      ch_scr[...] = jnp.zeros_like(ch_scr)

    prev3 = jnp.concatenate([cx_scr[...].reshape(1, 8, D_MODEL), x3[:nt - 1]], axis=0)
    cx_scr[...] = x_ref[rows - 8:rows, :]
    ct_ref[0] = x_ref[rows - 8:rows, :]
  else:
    prev3 = prev_ref[...].reshape(nt, 8, D_MODEL)

  sub = _iota((nt, 8, D_MODEL), 1)
  u = cb_ref[...] + cw_ref[3:4, :] * x
  for j in (1, 2, 3):
    xs = jnp.where(sub >= j, pltpu.roll(x3, j, 1), pltpu.roll(prev3, j, 1))
    u = u + cw_ref[3 - j:4 - j, :] * xs.reshape(rows, D_MODEL)

  gr, gi = [], []
  for g in range(N_GROUPS):
    ug = u[:, _gsl(g)].astype(BF16)
    gr.append(_dot(ug, wa_ref[g]))
    gi.append(_dot(ug, wx_ref[g]))
  nsplit = 4 if chain else 1
  prow = rows // nsplit
  pt = prow // 8
  sub = _iota((pt, 8, LANES), 1)
  row0 = _iota((prow, LANES), 0) == 0
  blocks = []
  for j in range(D_MODEL // LANES):
    cs = slice(j * LANES, (j + 1) * LANES)
    gs = slice((j % 2) * LANES, (j % 2 + 1) * LANES)
    hc = ch_scr[7:8, cs] if chain else None
    parts = []
    for part in range(nsplit):
      yield
      rs = slice(part * prow, (part + 1) * prow)
      gate_r = jax.nn.sigmoid(gr[j // 2][rs, gs] + ba_ref[:, cs])
      gate_i = jax.nn.sigmoid(gi[j // 2][rs, gs] + bx_ref[:, cs])
      log_a = (-LRU_C * gate_r) * _softplus(ap_ref[:, cs])
      a = jnp.exp(log_a)
      mult = jnp.sqrt(-_expm1(2.0 * log_a))
      if chain and part == 0:
        mult = jnp.where(jnp.logical_and(row0, c == 0), 1.0, mult)
      xin = u[rs, cs] * gate_i * mult

      a3 = a.reshape(pt, 8, LANES)
      b3 = xin.reshape(pt, 8, LANES)
      for s in (1, 2, 4):
        m = sub >= s
        b3 = jnp.where(m, a3 * pltpu.roll(b3, s, 1) + b3, b3)
        a3 = jnp.where(m, a3 * pltpu.roll(a3, s, 1), a3)
      if chain:
        hs = []
        for i in range(pt):
          hi = a3[i] * hc + b3[i]
          hs.append(hi)
          hc = hi[7:8, :]
        h = jnp.concatenate(hs, axis=0)
        if part == nsplit - 1:
          ch_scr[:, cs] = hs[-1]
          hl_ref[0, :, cs] = hc
      else:
        h3 = a3 * h0_ref[:, :, cs] + b3
        hl_ref[:, :, cs] = h3[:, 7:8, :]
        h = h3.reshape(prow, LANES)
      parts.append(jax.nn.sigmoid(ga_ref[rs, cs]) * (h * _gelu_tanh(y_ref[rs, cs])))
    blocks.append(_cat(parts, 0))
  return jnp.concatenate(blocks, axis=1)


def _lru_step(xy, gate_a, prev, h0, prm, l, *, nseq, tlen, nb=16):
  rows = nb * tlen
  row = lambda col: pl.BlockSpec((rows, D_MODEL), lambda b: (b, col))
  seq = pl.BlockSpec((nb, 1, D_MODEL), lambda b: (b, 0, 0))
  vec = pl.BlockSpec((None, 1, D_MODEL), lambda b: (l, 0, 0))
  gates_w = pl.BlockSpec((None, N_GROUPS, GROUP, GROUP), lambda b: (l, 0, 0, 0))
  return pl.pallas_call(
      _lru_step_kernel,
      grid=(nseq // nb,),
      in_specs=[row(0), row(1), row(0), row(0), seq,
                pl.BlockSpec((None, 4, D_MODEL), lambda b: (l, 0, 0)), vec, gates_w, vec, gates_w, vec, vec],
      out_specs=[row(0), seq],
      out_shape=[jax.ShapeDtypeStruct((nseq * tlen, D_MODEL), F32), jax.ShapeDtypeStruct((nseq, 1, D_MODEL), F32)],
      compiler_params=_params(("arbitrary",)),
      name="lru_step",
  )(xy, xy, gate_a, prev, h0, prm["conv_w"], prm["conv_b"], prm["wa_bd"], prm["lru_ba"], prm["wx_bd"],
    prm["lru_bx"], prm["lru_a_param"])


def _wkv_masks(tlen):
  lt = tlen.bit_length() - 1
  ri = _iota((CHUNK, GROUP), 0)
  cm = _iota((CHUNK, GROUP), 1) & (CHUNK - 1)
  same = (ri >> lt) == (cm >> lt)
  lane = _iota((CHUNK, LANES), 1)
  half = [jnp.where(lane < HEAD, 1.0, 0.0), jnp.where(lane >= HEAD, 1.0, 0.0)]
  return dict(
      strict=jnp.logical_and(same, cm < ri),
      incl=jnp.logical_and(same, cm <= ri),
      eye=jnp.where(cm == ri, 1.0, 0.0),
      half=half,
      half_b=[h.astype(BF16) for h in half],
      zero_b=jnp.zeros((CHUNK, LANES), BF16),
      ones_bd=_ones_bd(),
      colseq=(_iota((GROUP, 2 * CHUNK), 1) & (CHUNK - 1)) >> lt,
      lt=lt,
  )


def _bd_from_tiles(tiles, mk):
  z = mk["zero_b"]
  return jnp.concatenate([
      jnp.concatenate([tiles[0], z], axis=1), jnp.concatenate([tiles[1], z], axis=1),
      jnp.concatenate([z, tiles[2]], axis=1), jnp.concatenate([z, tiles[3]], axis=1)], axis=0)


def _bd(x, mk):
  xb = x.astype(BF16)
  return _bd_from_tiles([xb[:, (i // 2) * LANES:(i // 2 + 1) * LANES] * mk["half_b"][i % 2] for i in range(4)],
                        mk)


def _wkv_kernel(chain, tlen, cps, *refs):
  if chain:
    ops_ref, dt_ref, gg_ref, gbon_ref = refs[:4]
    lru_in = refs[4:14]
    gnw_ref, gnb_ref, out_ref, sout_ref, hl_ref, ct_ref, s_scr, cx_scr, ch_scr = refs[14:]
    lru = _Staged(_lru_body(True, *lru_in, hl_ref, ct_ref, cx_scr, ch_scr))
    lru.tick()
    tick = lru.tick
  else:
    tick = lambda: None
    ops_ref, dt_ref, pa_ref, gg_ref, gbon_ref, s0_ref, gnw_ref, gnb_ref, out_ref, sout_ref = refs
    pa = pa_ref[...]
  nsq = CHUNK // tlen
  mk = _wkv_masks(tlen)
  groups = range(N_GROUPS)
  units = [(ci, gi) for ci in range(cps) for gi in groups]
  cut = lambda ref, un: ref[un[0] * CHUNK:(un[0] + 1) * CHUNK, _gsl(un[1])]
  op = lambda idx, un: ops_ref[idx, un[0] * CHUNK:(un[0] + 1) * CHUNK, _gsl(un[1])]
  tile_of = lambda i: slice((i // 2) * LANES, (i // 2 + 1) * LANES)
  rblk = lambda i: slice(i * HEAD, (i + 1) * HEAD)

  if chain:
    c = pl.program_id(1)

    @pl.when(c == 0)
    def _():
      s_scr[...] = jnp.zeros_like(s_scr)

  lhs, n_ab, a_ak, a_rbk = {}, {}, {}, {}
  for idx, un in enumerate(units):
    lhs[un] = jnp.concatenate([op(0, un), op(1, un)], axis=0)
    gb = _dot(lhs[un], _bd(op(2, un), mk), trans_b=True)
    gk = _dot(lhs[un], _bd(op(3, un), mk), trans_b=True)
    n_ab[un] = jnp.where(mk["strict"], gb[:CHUNK], 0.0)
    a_ak[un] = jnp.where(mk["strict"], gk[:CHUNK], 0.0)
    a_rbk[un] = jnp.concatenate([jnp.where(mk["incl"], gb[CHUNK:], 0.0),
                                 jnp.where(mk["incl"], gk[CHUNK:], 0.0)], axis=1)
    if idx % 2:
      tick()

  t = {un: mk["eye"] + n_ab[un] for un in units}
  q = {un: _dot(n_ab[un], _bd(n_ab[un], mk)) for un in units}
  tick()
  levels = mk["lt"]
  for lvl in range(1, levels):
    for idx, un in enumerate(units):
      bq = _bd(q[un], mk)
      if lvl + 1 < levels:
        tq = _dot(jnp.concatenate([t[un], q[un]], axis=0), bq)
        t[un] = t[un] + tq[:CHUNK]
        q[un] = tq[CHUNK:]
      else:
        t[un] = t[un] + _dot(t[un], bq)
      if idx % 2:
        tick()

  bdv, a_v = {}, {}
  for idx, un in enumerate(units):
    bdv[un] = _bd(op(6, un), mk)
    a_v[un] = _dot(a_ak[un], bdv[un])
    if idx % 2:
      tick()
  lane_half = lambda i: slice((i % 2) * HEAD, (i % 2 + 1) * HEAD)
  o, gn = {}, {}

  def group_norm_staged(og, gi):
    o_mu = _dot(og, mk["ones_bd"]) * (1.0 / HEAD)
    yield
    oc = og - o_mu
    o_var = _dot(oc * oc, mk["ones_bd"]) * (1.0 / HEAD)
    return oc * lax.rsqrt(o_var + GN_EPS) * gnw_ref[:, _gsl(gi)] + gnb_ref[:, _gsl(gi)]

  def group_norm(og, gi):
    return _Staged(group_norm_staged(og, gi)).finish()

  if chain:
    xt, dcol = {}, {}
    for un in units:
      dt = cut(dt_ref, un)
      xpt = jnp.concatenate([op(4, un).astype(F32), op(5, un).astype(F32), dt, dt], axis=0).T
      xt[un] = xpt[:, :2 * CHUNK]
      dcol[un] = xpt[:, 2 * CHUNK:]
    p_cur = {gi: [s_scr[gi, i] for i in range(4)] for gi in groups}
    for ci in range(cps):
      a_p, r_p, u = {}, {}, {}
      for gi in groups:
        lp = _dot(lhs[(ci, gi)], _bd_from_tiles([tl.astype(BF16) for tl in p_cur[gi]], mk))
        a_p[gi], r_p[gi] = lp[:CHUNK], lp[CHUNK:]
      tick()
      if ci > 0:
        pend = [_Staged(group_norm_staged(o[(ci - 1, gi)], gi)) for gi in groups]
        for st in pend:
          st.tick()
      for gi in groups:
        u[gi] = _dot(t[(ci, gi)], _bd(a_p[gi] + a_v[(ci, gi)], mk))
      if ci > 0:
        for gi, st in zip(groups, pend):
          gn[(ci - 1, gi)] = st.finish()
      tick()
      for gi in groups:
        un = (ci, gi)
        y = jnp.concatenate([u[gi].astype(BF16), op(6, un)], axis=0)
        upd = _dot(xt[un], y)
        p_cur[gi] = [(p_cur[gi][i] * dcol[un][rblk(i)] + upd[rblk(i), tile_of(i)]) * mk["half"][i % 2]
                     for i in range(4)]
      tick()
      for gi in groups:
        un = (ci, gi)
        o[un] = r_p[gi] + _dot(a_rbk[un], jnp.concatenate([_bd(u[gi], mk), bdv[un]], axis=0))
      tick()
    for gi in groups:
      gn[(cps - 1, gi)] = group_norm(o[(cps - 1, gi)], gi)
    for gi in groups:
      for i in range(4):
        s_scr[gi, i] = p_cur[gi][i]

    @pl.when(c == pl.num_programs(1) - 1)
    def _():
      for gi in groups:
        for i in range(4):
          sout_ref[0, 4 * gi + i] = s_scr[gi, i][:, lane_half(i)].T
  else:
    zero = jnp.zeros((HEAD, HEAD), F32)
    tiles, a_p, r_p, u = {}, {}, {}, {}
    for un in units:
      ci, gi = un
      tiles[un], aps, rps = [], [], []
      for j in range(nsq):
        st = [s0_ref[ci * nsq + j, 4 * gi + i] for i in range(4)]
        tiles[un].append([jnp.concatenate([st[i], zero] if i % 2 == 0 else [zero, st[i]], axis=1)
                          for i in range(4)])
        lj = jnp.concatenate([lhs[un][j * tlen:(j + 1) * tlen],
                              lhs[un][CHUNK + j * tlen:CHUNK + (j + 1) * tlen]], axis=0)
        lpj = _dot(lj, _bd_from_tiles([tl.astype(BF16) for tl in tiles[un][j]], mk), trans_b=True)
        aps.append(lpj[:tlen])
        rps.append(lpj[tlen:])
      a_p[un] = jnp.concatenate(aps, axis=0)
      r_p[un] = jnp.concatenate(rps, axis=0)
    for un in units:
      u[un] = _dot(t[un], _bd(a_p[un] + a_v[un], mk))
    for un in units:
      o[un] = r_p[un] + _dot(a_rbk[un], jnp.concatenate([_bd(u[un], mk), bdv[un]], axis=0))
    for un in units:
      ci, gi = un
      yt = jnp.concatenate([u[un], op(6, un).astype(F32)], axis=0).T
      x = jnp.concatenate([op(4, un), op(5, un)], axis=0)
      dt = cut(dt_ref, un)
      for j in range(nsq):
        upd = _dot(jnp.where(mk["colseq"] == j, yt, 0.0), x)
        for i in range(4):
          new = tiles[un][j][i] * dt[j * tlen:j * tlen + 1, tile_of(i)] + upd[rblk(i), tile_of(i)]
          sout_ref[ci * nsq + j, 4 * gi + i] = new[:, lane_half(i)]

  if chain:
    outs = [_cat([gn[(ci, gi)] for ci in range(cps)], 0) for gi in groups]
    pa = lru.finish()
  else:
    outs = [group_norm(_cat([o[(ci, gi)] for ci in range(cps)], 0), gi) for gi in groups]
  out_ref[...] = pa + gbon_ref[...] + gg_ref[...] * jnp.concatenate(outs, axis=1)


N_HEADS = 4 * N_GROUPS


def _wkv_chain(ops, dtot, gg, gbon, xy, gate_a, prm, l, *, nseq, tlen, cps=2):
  rows = cps * CHUNK
  nstep = tlen // rows
  rmap = lambda b, c: (b * nstep + c, 0)
  row = pl.BlockSpec((rows, D_MODEL), rmap)
  vec = pl.BlockSpec((None, 1, D_MODEL), lambda b, c: (l, 0, 0))
  gates_w = pl.BlockSpec((None, N_GROUPS, GROUP, GROUP), lambda b, c: (l, 0, 0, 0))
  per_seq = lambda r, w: pl.BlockSpec((1, r, w), lambda b, c: (b, 0, 0))
  in_specs = [pl.BlockSpec((N_WKV_OPS, rows, D_MODEL), lambda b, c: (0, b * nstep + c, 0)), row, row, row,
              row, pl.BlockSpec((rows, D_MODEL), lambda b, c: (b * nstep + c, 1)), row,
              pl.BlockSpec((None, 4, D_MODEL), lambda b, c: (l, 0, 0)), vec, gates_w, vec, gates_w, vec, vec,
              vec, vec]
  args = [ops, dtot, gg, gbon, xy, xy, gate_a,
          prm["conv_w"], prm["conv_b"], prm["wa_bd"], prm["lru_ba"], prm["wx_bd"], prm["lru_bx"],
          prm["lru_a_param"], prm["gn_w"], prm["gn_b"]]
  return pl.pallas_call(
      functools.partial(_wkv_kernel, True, CHUNK, cps),
      grid=(nseq, nstep), in_specs=in_specs,
      out_specs=[row, pl.BlockSpec((1, N_HEADS, HEAD, HEAD), lambda b, c: (b, 0, 0, 0)),
                 per_seq(1, D_MODEL), per_seq(8, D_MODEL)],
      out_shape=[jax.ShapeDtypeStruct((nseq * tlen, D_MODEL), F32),
                 jax.ShapeDtypeStruct((nseq, N_HEADS, HEAD, HEAD), F32),
                 jax.ShapeDtypeStruct((nseq, 1, D_MODEL), F32), jax.ShapeDtypeStruct((nseq, 8, D_MODEL), F32)],
      scratch_shapes=[pltpu.VMEM((N_GROUPS, 4, HEAD, LANES), F32), pltpu.VMEM((8, D_MODEL), F32),
                      pltpu.VMEM((8, D_MODEL), F32)],
      compiler_params=_params(("arbitrary", "arbitrary")),
      name="wkv_chain",
  )(*args)


def _wkv_step(ops, dtot, pa, gg, gbon, states, prm, l, *, nseq, tlen, cps=2):
  rows = cps * CHUNK
  sblk = rows // tlen
  row = pl.BlockSpec((rows, D_MODEL), lambda b, c: (b, 0))
  vec = pl.BlockSpec((None, 1, D_MODEL), lambda b, c: (l, 0, 0))
  sspec = pl.BlockSpec((None, sblk, N_HEADS, HEAD, HEAD), lambda b, c: (l, b, 0, 0, 0))
  in_specs = [pl.BlockSpec((N_WKV_OPS, rows, D_MODEL), lambda b, c: (0, b, 0)), row, row, row, row, sspec,
              vec, vec]
  return pl.pallas_call(
      functools.partial(_wkv_kernel, False, tlen, cps),
      grid=(nseq // sblk, 1), in_specs=in_specs,
      out_specs=[row, sspec],
      out_shape=[jax.ShapeDtypeStruct((nseq * tlen, D_MODEL), F32),
                 jax.ShapeDtypeStruct(states.shape, F32)],
      input_output_aliases={5: 1},
      compiler_params=_params(("arbitrary", "arbitrary")),
      name="wkv_step",
  )(ops, dtot, pa, gg, gbon, states, prm["gn_w"], prm["gn_b"])


def _mlp_kernel(x_ref, m_ref, wo_ref, g1_ref, b1_ref, w1_ref, w2_ref, g2_ref, b2_ref,
                o_ref, x1_scr, x1b_scr, acc_scr):
  j = pl.program_id(1)
  last = pl.num_programs(1) - 1
  sub = 256
  nb = x_ref.shape[0] // sub
  rb = lambda b: slice(b * sub, (b + 1) * sub)

  def outproj(b):
    return DN_ALPHA * x_ref[rb(b), :] + _dot(m_ref[rb(b), :], wo_ref[...])

  def ln1(b, y):
    x1 = _layer_norm(y, g1_ref[...], b1_ref[...])
    x1_scr[rb(b), :] = x1
    x1b_scr[rb(b), :] = x1.astype(BF16)

  def ff(b, first):
    hid = jnp.square(jnp.maximum(_dot(x1b_scr[rb(b), :], w1_ref[...]), 0.0))
    part = _dot(hid, w2_ref[...])
    acc_scr[rb(b), :] = part if first else acc_scr[rb(b), :] + part

  def ln2(b):
    o_ref[rb(b), :] = _layer_norm(DN_ALPHA * x1_scr[rb(b), :] + acc_scr[rb(b), :], g2_ref[...], b2_ref[...])

  @pl.when(j == 0)
  def _():
    y = outproj(0)
    for b in range(nb):
      y_next = outproj(b + 1) if b + 1 < nb else None
      ln1(b, y)
      if b > 0:
        ff(b - 1, True)
      y = y_next
    ff(nb - 1, True)

  @pl.when(jnp.logical_and(j > 0, j < last))
  def _():
    for b in range(nb):
      ff(b, False)

  @pl.when(j == last)
  def _():
    for b in range(nb):
      ff(b, False)
      if b > 0:
        ln2(b - 1)
    ln2(nb - 1)


def _mlp(x, merged, prm, l, tm=512, tf=1024):
  n = x.shape[0]
  vec = pl.BlockSpec((None, 1, D_MODEL), lambda i, j: (l, 0, 0))
  row = pl.BlockSpec((tm, D_MODEL), lambda i, j: (i, 0))
  return pl.pallas_call(
      _mlp_kernel,
      grid=(n // tm, D_FF // tf),
      in_specs=[
          row, row,
          pl.BlockSpec((None, D_MODEL, D_MODEL), lambda i, j: (l, 0, 0)),
          vec, vec,
          pl.BlockSpec((None, D_MODEL, tf), lambda i, j: (l, 0, j)),
          pl.BlockSpec((None, tf, D_MODEL), lambda i, j: (l, j, 0)),
          vec, vec,
      ],
      out_specs=row,
      out_shape=jax.ShapeDtypeStruct((n, D_MODEL), F32),
      scratch_shapes=[pltpu.VMEM((tm, D_MODEL), F32), pltpu.VMEM((tm, D_MODEL), BF16),
                      pltpu.VMEM((tm, D_MODEL), F32)],
      compiler_params=_params(("arbitrary", "arbitrary")),
      name="outproj_mlp",
  )(x, merged, prm["w_out"], prm["ln1_g"], prm["ln1_b"], prm["mlp_w1"], prm["mlp_w2"],
    prm["ln2_g"], prm["ln2_b"])


def _block_diag4(w):
  depth = w.shape[0]
  w5 = w.reshape(depth, N_GROUPS, 4, HEAD, HEAD)
  eye = jnp.eye(4, dtype=w.dtype)
  return jnp.einsum("lgaij,ab->lgaibj", w5, eye).reshape(depth, N_GROUPS, GROUP, GROUP)


def kernel(x_prompt, x_sample, state_conv, state_lru, state_shift, state_wkv, w_in, conv_w, conv_b, lru_wa,
           lru_ba, lru_wx, lru_bx, lru_a_param, shift_mu, decay_up, w0, aaa_up, a0, gate_up, k_k, k_a, r_k,
           gn_w, gn_b, w_out, ln1_g, ln1_b, mlp_w1, mlp_w2, ln2_g, ln2_b):
  bp, tp, _ = x_prompt.shape
  bs, ts, _ = x_sample.shape
  depth = w_in.shape[0]

  row = lambda p: p.reshape(depth, 1, -1)
  w_in_b = w_in.astype(BF16)
  zero = jnp.zeros((depth, 64, D_MODEL), F32)
  lora_w = jnp.concatenate([jnp.concatenate([decay_up, zero], axis=2),
                            jnp.concatenate([zero, aaa_up], axis=2)], axis=1).astype(BF16)
  prm = dict(
      conv_w=conv_w, conv_b=row(conv_b),
      wa_bd=_block_diag4(lru_wa).astype(BF16), wx_bd=_block_diag4(lru_wx).astype(BF16),
      lru_ba=row(lru_ba), lru_bx=row(lru_bx), lru_a_param=row(lru_a_param),
      shift_mu=row(shift_mu), lora_w=lora_w, w0=row(w0), a0=row(a0), gate_up=gate_up.astype(BF16),
      k_k=row(k_k), k_a=row(k_a), r_k=row(r_k), gn_w=row(gn_w), gn_b=row(gn_b),
      w_out=w_out.astype(BF16), ln1_g=row(ln1_g), ln1_b=row(ln1_b),
      mlp_w1=mlp_w1.astype(BF16), mlp_w2=mlp_w2.astype(BF16), ln2_g=row(ln2_g), ln2_b=row(ln2_b),
  )

  xp = x_prompt.reshape(bp * tp, D_MODEL)
  xs = x_sample.reshape(bs * ts, D_MODEL)
  conv_p, lru_p, shift_p, wkv_p = [], [], [], []
  conv_s, lru_s, shift_s = [], [], []
  wkv_s = state_wkv
  for l in range(depth):
    xy, ga, ops, dtot, gg, gbon, stail = _front(xp, w_in_b, prm, l, chain=True, nseq=bp, tlen=tp)
    mg, z, hl, ctail = _wkv_chain(ops, dtot, gg, gbon, xy, ga, prm, l, nseq=bp, tlen=tp)
    xp = _mlp(xp, mg, prm, l, tm=1024)
    conv_p.append(ctail[:, 5:, :])
    lru_p.append(hl.reshape(bp, D_MODEL))
    shift_p.append(stail[:, 7, :])
    wkv_p.append(z)

    xy, ga, ops, dtot, gg, gbon, sh = _front(xs, w_in_b, prm, l, chain=False, nseq=bs, tlen=ts, tm=128,
                                             sprev=jnp.repeat(state_shift[l], ts, axis=0))
    prev = jnp.pad(state_conv[l], ((0, 0), (5, 0), (0, 0))).reshape(bs * 8, D_MODEL)
    pa, hl = _lru_step(xy, ga, prev, state_lru[l].reshape(bs, 1, D_MODEL), prm, l, nseq=bs, tlen=ts)
    mg, wkv_s = _wkv_step(ops, dtot, pa, gg, gbon, wkv_s, prm, l, nseq=bs, tlen=ts)
    xs = _mlp(xs, mg, prm, l, tm=1024)
    conv_s.append(xy.reshape(bs, ts, XY_WIDTH)[:, ts - 3:, :D_MODEL])
    lru_s.append(hl.reshape(bs, D_MODEL))
    shift_s.append(sh.reshape(bs, ts, SHIFT_WIDTH)[:, ts - 1])

  return (xp.reshape(bp, tp, D_MODEL), xs.reshape(bs, ts, D_MODEL),
          jnp.stack(conv_p), jnp.stack(lru_p), jnp.stack(shift_p), jnp.stack(wkv_p),
          jnp.stack(conv_s), jnp.stack(lru_s), jnp.stack(shift_s), wkv_s)
```

```python
import functools
import math

import jax
import jax.numpy as jnp
from jax import lax
from jax.experimental import pallas as pl
from jax.experimental.pallas import tpu as pltpu

F32 = jnp.float32
BF16 = jnp.bfloat16

D_MODEL = 1024
DEPTH = 4
LRU_C = 8.0
HEAD = 64
GROUP = 256
LANES = 128
N_GROUPS = D_MODEL // GROUP
SHIFT_WIDTH = 3 * D_MODEL + 64 + 64 + 128
XY_WIDTH = 2 * D_MODEL
IN_COLS = 2 * XY_WIDTH + SHIFT_WIDTH
D_FF = 4 * D_MODEL
DN_ALPHA = (2 * DEPTH) ** 0.25
LN_EPS = 1e-5
GN_EPS = 64e-5
CHUNK = 64
VMEM_LIMIT = 56 * 1024 * 1024
NP_CUM = 2
N_WKV_OPS = 7


def _split(x, n):
  pieces = []
  rem = x
  for i in range(n):
    p = rem.astype(BF16)
    pieces.append(p)
    if i + 1 < n:
      rem = rem - p.astype(F32)
  return pieces


def _dot(a, b, trans_b=False):
  dims = (((1,), (1,)), ((), ())) if trans_b else (((1,), (0,)), ((), ()))
  return lax.dot_general(a.astype(BF16), b.astype(BF16), dims, preferred_element_type=F32)


def _softplus(x):
  return jnp.maximum(x, 0.0) + jnp.log1p(jnp.exp(-jnp.abs(x)))


def _neg_expm1_2x(x, ex):
  return -jnp.tanh(x) * (ex * ex + 1.0)


def _gelu_tanh(x):
  c = math.sqrt(2.0 / math.pi)
  return x * (0.5 * (1.0 + jnp.tanh(c * (x + 0.044715 * (x * x * x)))))


def _layer_norm(x, g, b):
  mu = jnp.mean(x, axis=-1, keepdims=True)
  xc = x - mu
  var = jnp.mean(xc * xc, axis=-1, keepdims=True)
  return xc * lax.rsqrt(var + LN_EPS) * g + b


def _iota(shape, dim):
  return lax.broadcasted_iota(jnp.int32, shape, dim)


def _cat(xs, axis):
  return xs[0] if len(xs) == 1 else jnp.concatenate(xs, axis=axis)


def _params(sem):
  return pltpu.CompilerParams(dimension_semantics=sem, vmem_limit_bytes=VMEM_LIMIT)


def _ones_bd():
  same = (_iota((GROUP, GROUP), 0) >> 6) == (_iota((GROUP, GROUP), 1) >> 6)
  return jnp.where(same, 1.0, 0.0).astype(BF16)


def _gsl(gi):
  return slice(gi * GROUP, (gi + 1) * GROUP)


def _front_dots(x_ref, w_ref, xy_ref, ga_ref):
  xb = x_ref[...].astype(BF16)
  sh = _dot(xb, w_ref[:, XY_WIDTH:XY_WIDTH + SHIFT_WIDTH])
  xy_ref[...] = _dot(xb, w_ref[:, 0:XY_WIDTH])
  gt = _dot(xb, w_ref[:, XY_WIDTH + SHIFT_WIDTH:IN_COLS])
  ga_ref[...] = gt[:, :D_MODEL]
  return sh, gt[:, D_MODEL:]


def _front_prep(sh, gate_b, prev, tlen, prm_refs, out_refs, r0=0):
  mu_ref, lw_ref, w0_ref, a0_ref, gup_ref, kk_ref, ka_ref, rk_ref = prm_refs
  ops_ref, dt_ref, gg_ref, gbon_ref = out_refs
  rows = sh.shape[0]
  rsl = slice(r0, r0 + rows)
  nchunk = rows // CHUNK
  lt = tlen.bit_length() - 1
  ones_bd = _ones_bd()
  headsum = lambda z: _cat([_dot(z[:, _gsl(gi)], ones_bd) for gi in range(N_GROUPS)], 1)

  mixed = sh + (prev - sh) * mu_ref[...]
  r = mixed[:, 0:D_MODEL]
  k = mixed[:, D_MODEL:2 * D_MODEL]
  v = mixed[:, 2 * D_MODEL:3 * D_MODEL]
  xl = mixed[:, 3 * D_MODEL:3 * D_MODEL + 128]
  xg = mixed[:, 3 * D_MODEL + 128:SHIFT_WIDTH]
  lane = _iota((rows, 128), 1)
  lo = _dot(jnp.where(lane < 64, jnp.tanh(xl), xl), lw_ref[...])
  g = _dot(jax.nn.sigmoid(xg), gup_ref[...])
  kk = k * kk_ref[...]
  ss = headsum(kk * kk)
  yield

  w_log = -_softplus(-(w0_ref[...] + lo[:, :D_MODEL])) - 0.5
  w = -jnp.exp(w_log)
  a = jax.nn.sigmoid(a0_ref[...] + lo[:, D_MODEL:])
  k_h = k * (1.0 + (a - 1.0) * ka_ref[...])
  rkr = r * k_h * rk_ref[...]
  bonus_sum = headsum(rkr)
  rs = _iota((2 * CHUNK, CHUNK), 0)
  cs = _iota((2 * CHUNK, CHUNK), 1)
  rr = rs & (CHUNK - 1)
  cum_mask = jnp.logical_and((rr >> lt) == (cs >> lt), jnp.logical_or(cs <= rr, rs >= CHUNK))
  cum_mask = jnp.where(cum_mask, 1.0, 0.0).astype(BF16)
  one_seq = tlen == CHUNK
  if one_seq:
    cum_mask = cum_mask[:CHUNK]
  cums, tots = [], []
  for ci in range(nchunk):
    pieces = _split(w[ci * CHUNK:(ci + 1) * CHUNK], NP_CUM)
    cw = _dot(cum_mask, pieces[0])
    for p in pieces[1:]:
      cw = cw + _dot(cum_mask, p)
    cums.append(cw[:CHUNK])
    tots.append(jnp.broadcast_to(cw[CHUNK - 1:CHUNK], (CHUNK, D_MODEL)) if one_seq else cw[CHUNK:])
  yield

  kkn = kk / jnp.maximum(jnp.sqrt(ss), 1e-12)
  bv = kkn * a
  gg = jax.nn.sigmoid(gate_b) * g
  gg_ref[rsl, :] = gg
  gbon_ref[rsl, :] = gg * (bonus_sum * v)
  yield
  cum = _cat(cums, 0)
  tot = _cat(tots, 0)
  e_neg = jnp.exp(-cum)
  e_rem = jnp.exp(tot - cum)
  ops_ref[0, rsl, :] = (-kkn * jnp.exp(cum - w)).astype(BF16)
  ops_ref[1, rsl, :] = (r * jnp.exp(cum)).astype(BF16)
  ops_ref[2, rsl, :] = (bv * e_neg).astype(BF16)
  ops_ref[3, rsl, :] = (k_h * e_neg).astype(BF16)
  yield
  ops_ref[4, rsl, :] = (bv * e_rem).astype(BF16)
  ops_ref[5, rsl, :] = (k_h * e_rem).astype(BF16)
  ops_ref[6, rsl, :] = v.astype(BF16)
  dt_ref[rsl, :] = jnp.exp(tot)


def _front_prep_fine(mix, gate_b, prm_refs, out_refs, r0):
  mu_ref, lw_ref, w0_ref, a0_ref, gup_ref, kk_ref, ka_ref, rk_ref = prm_refs
  ops_ref, dt_ref, gg_ref, gbon_ref = out_refs
  rsl = slice(r0, r0 + CHUNK)
  ones_bd = _ones_bd()
  cum_mask = jnp.where(_iota((CHUNK, CHUNK), 1) <= _iota((CHUNK, CHUNK), 0), 1.0, 0.0).astype(BF16)
  lane = _iota((CHUNK, LANES), 1)
  col = lambda base, gi: slice(base + gi * GROUP, base + (gi + 1) * GROUP)

  xl = mix(slice(3 * D_MODEL, 3 * D_MODEL + LANES))
  xg = mix(slice(3 * D_MODEL + LANES, SHIFT_WIDTH))
  lo = _dot(jnp.where(lane < 64, jnp.tanh(xl), xl), lw_ref[...])
  g = _dot(jax.nn.sigmoid(xg), gup_ref[...])
  yield
  st = {}
  for gi in range(N_GROUPS):
    k = mix(col(D_MODEL, gi))
    kk = k * kk_ref[:, _gsl(gi)]
    st[gi] = dict(k=k, kk=kk, ss=_dot(kk * kk, ones_bd))
    yield
  for gi in range(N_GROUPS):
    cs, d = _gsl(gi), st[gi]
    w = -jnp.exp(-_softplus(-(w0_ref[:, cs] + lo[:, cs])) - 0.5)
    a = jax.nn.sigmoid(a0_ref[:, cs] + lo[:, col(D_MODEL, gi)])
    k_h = d["k"] * (1.0 + (a - 1.0) * ka_ref[:, cs])
    r = mix(col(0, gi))
    bsum = _dot(r * k_h * rk_ref[:, cs], ones_bd)
    pieces = _split(w, NP_CUM)
    cum = _dot(cum_mask, pieces[0])
    for p in pieces[1:]:
      cum = cum + _dot(cum_mask, p)
    d.update(w=w, a=a, k_h=k_h, r=r, bsum=bsum, cum=cum)
    yield
  for gi in range(N_GROUPS):
    cs, d = _gsl(gi), st[gi]
    v = mix(col(2 * D_MODEL, gi))
    kkn = d["kk"] / jnp.maximum(jnp.sqrt(d["ss"]), 1e-12)
    bv = kkn * d["a"]
    gg = jax.nn.sigmoid(gate_b(cs)) * g[:, cs]
    gg_ref[rsl, cs] = gg
    gbon_ref[rsl, cs] = gg * (d["bsum"] * v)
    cum = d["cum"]
    tot = jnp.broadcast_to(cum[CHUNK - 1:CHUNK], cum.shape)
    e_neg = jnp.exp(-cum)
    e_rem = jnp.exp(tot - cum)
    ops_ref[0, rsl, cs] = (-kkn * jnp.exp(cum - d["w"])).astype(BF16)
    ops_ref[1, rsl, cs] = (d["r"] * jnp.exp(cum)).astype(BF16)
    ops_ref[2, rsl, cs] = (bv * e_neg).astype(BF16)
    ops_ref[3, rsl, cs] = (d["k_h"] * e_neg).astype(BF16)
    ops_ref[4, rsl, cs] = (bv * e_rem).astype(BF16)
    ops_ref[5, rsl, cs] = (d["k_h"] * e_rem).astype(BF16)
    ops_ref[6, rsl, cs] = v.astype(BF16)
    dt_ref[rsl, cs] = jnp.exp(tot)
    yield


def _front_step_kernel(tlen, x_ref, w_ref, sp_ref, *refs):
  prm_refs, (xy_ref, ga_ref), out_refs, sh_ref = refs[:8], refs[8:10], refs[10:14], refs[14]
  sh, gate_b = _front_dots(x_ref, w_ref, xy_ref, ga_ref)
  sh_ref[...] = sh
  row = _iota(sh.shape, 0)
  prev = jnp.where((row & (tlen - 1)) == 0, sp_ref[...], pltpu.roll(sh, 1, 0))
  for _ in _front_prep(sh, gate_b, prev, tlen, prm_refs, out_refs):
    pass


def _front_chain_kernel(nstep, ntiles, x_ref, w_ref, *refs):
  prm_refs, (xy_ref, ga_ref), out_refs = refs[:8], refs[8:10], refs[10:14]
  st_ref, sh_a, sh_b, gb_a, gb_b, carry_scr = refs[14:]
  s = pl.program_id(0)
  rows = x_ref.shape[0]
  bufs = ((sh_a, gb_a), (sh_b, gb_b))

  nchunk = rows // CHUNK
  step_w = GROUP
  pieces = [("xy", c, c + step_w) for c in range(0, XY_WIDTH, step_w)]
  pieces += [("sh", c, min(c + step_w, SHIFT_WIDTH)) for c in range(0, SHIFT_WIDTH, step_w)]
  pieces += [("ga", c, c + step_w) for c in range(0, D_MODEL, step_w)]
  pieces += [("gb", c, c + step_w) for c in range(0, D_MODEL, step_w)]
  w_base = {"xy": 0, "sh": XY_WIDTH, "ga": XY_WIDTH + SHIFT_WIDTH, "gb": XY_WIDTH + SHIFT_WIDTH + D_MODEL}

  def dot_piece(parity, piece):
    sh_scr, gb_scr = bufs[parity]
    dst, c0, c1 = piece
    res = _dot(x_ref[...].astype(BF16), w_ref[:, w_base[dst] + c0:w_base[dst] + c1])
    {"xy": xy_ref, "sh": sh_scr, "ga": ga_ref, "gb": gb_scr}[dst][:, c0:c1] = res

  def prep_piece(parity, ci):
    sh_scr, gb_scr = bufs[parity]
    r0 = ci * CHUNK
    row0 = _iota((CHUNK, LANES), 0) == 0
    seq_start = ((s - 1) % nstep) == 0

    def mix(cs):
      sh = sh_scr[r0:r0 + CHUNK, cs]
      first = jnp.where(seq_start, 0.0, carry_scr[7:8, cs]) if ci == 0 else sh_scr[r0 - 1:r0, cs]
      width = sh.shape[1]
      prev = jnp.where(jnp.concatenate([row0] * (width // LANES), axis=1), first, pltpu.roll(sh, 1, 0))
      return sh + (prev - sh) * mu_ref[:, cs]

    return _front_prep_fine(mix, lambda cs: gb_scr[r0:r0 + CHUNK, cs], prm_refs, out_refs, r0)

  def prep_all(parity, between=lambda done, total: None):
    total = nchunk * (1 + 3 * N_GROUPS)
    done = 0
    for ci in range(nchunk):
      for _ in prep_piece(parity, ci):
        done += 1
        between(done, total)
    tail = bufs[parity][0][rows - 8:rows, :]
    carry_scr[...] = tail
    st_ref[0] = tail

  mu_ref = prm_refs[0]

  @pl.when(s == 0)
  def _():
    for piece in pieces:
      dot_piece(0, piece)

  for parity in (0, 1):
    @pl.when(jnp.logical_and(jnp.logical_and(s > 0, s < ntiles), s % 2 == parity))
    def _(parity=parity):
      todo = list(pieces)

      def between(done, total):
        while todo and (len(pieces) - len(todo)) * total < done * len(pieces):
          dot_piece(parity, todo.pop(0))

      prep_all(1 - parity, between)
      for piece in todo:
        dot_piece(parity, piece)

  @pl.when(s == ntiles)
  def _():
    prep_all((ntiles - 1) % 2)


def _front(x, w_in_b, prm, l, *, chain, nseq, tlen, tm=256, sprev=None):
  n = x.shape[0]
  ntiles = n // tm
  if chain:
    nstep = tlen // tm
    grid = (ntiles + 1,)
    cur = lambda s: jnp.minimum(s, ntiles - 1)
    old = lambda s: jnp.maximum(s - 1, 0)
    body = functools.partial(_front_chain_kernel, nstep, ntiles)
  else:
    grid = (ntiles,)
    cur = old = lambda s: s
    body = functools.partial(_front_step_kernel, tlen)
  row = lambda width, which: pl.BlockSpec((tm, width), lambda s: (which(s), 0))
  vec = pl.BlockSpec((None, 1, D_MODEL), lambda s: (l, 0, 0))
  in_specs = [row(D_MODEL, cur), pl.BlockSpec((None, D_MODEL, IN_COLS), lambda s: (l, 0, 0),
                                              pipeline_mode=pl.Buffered(1))]
  args = [x, w_in_b]
  if not chain:
    in_specs.append(row(SHIFT_WIDTH, cur))
    args.append(sprev)
  in_specs += [pl.BlockSpec((None, 1, SHIFT_WIDTH), lambda s: (l, 0, 0)),
               pl.BlockSpec((None, 128, 2 * D_MODEL), lambda s: (l, 0, 0)),
               vec, vec,
               pl.BlockSpec((None, 128, D_MODEL), lambda s: (l, 0, 0)),
               vec, vec, vec]
  args += [prm["shift_mu"], prm["lora_w"], prm["w0"], prm["a0"], prm["gate_up"], prm["k_k"], prm["k_a"],
           prm["r_k"]]
  out_specs = [row(XY_WIDTH, cur), row(D_MODEL, cur),
               pl.BlockSpec((N_WKV_OPS, tm, D_MODEL), lambda s: (0, old(s), 0)),
               row(D_MODEL, old), row(D_MODEL, old), row(D_MODEL, old)]
  f32 = lambda width: jax.ShapeDtypeStruct((n, width), F32)
  out_shape = [f32(XY_WIDTH), f32(D_MODEL), jax.ShapeDtypeStruct((N_WKV_OPS, n, D_MODEL), BF16),
               f32(D_MODEL), f32(D_MODEL), f32(D_MODEL)]
  scratch = []
  if chain:
    out_specs.append(pl.BlockSpec((1, 8, SHIFT_WIDTH), lambda s: (old(s) // nstep, 0, 0)))
    out_shape.append(jax.ShapeDtypeStruct((nseq, 8, SHIFT_WIDTH), F32))
    scratch = [pltpu.VMEM((tm, SHIFT_WIDTH), F32), pltpu.VMEM((tm, SHIFT_WIDTH), F32),
               pltpu.VMEM((tm, D_MODEL), F32), pltpu.VMEM((tm, D_MODEL), F32),
               pltpu.VMEM((8, SHIFT_WIDTH), F32)]
  else:
    out_specs.append(row(SHIFT_WIDTH, cur))
    out_shape.append(f32(SHIFT_WIDTH))
  return pl.pallas_call(
      body, grid=grid, in_specs=in_specs, out_specs=out_specs, out_shape=out_shape, scratch_shapes=scratch,
      compiler_params=_params(("arbitrary",)),
      name="front_chain" if chain else "front_step",
  )(*args)


class _Staged:
  def __init__(self, gen):
    self.gen, self.value, self.done = gen, None, False

  def tick(self):
    if not self.done:
      try:
        next(self.gen)
      except StopIteration as stop:
        self.value, self.done = stop.value, True

  def finish(self):
    while not self.done:
      self.tick()
    return self.value


def _lru_step_kernel(*refs):
  n_in = 12
  refs[n_in][...] = _Staged(_lru_body(False, *refs[:n_in], *refs[n_in + 1:])).finish()


def _lru_body(chain, *refs):
  if chain:
    (x_ref, y_ref, ga_ref, cw_ref, cb_ref, wa_ref, ba_ref, wx_ref, bx_ref, ap_ref,
     hl_ref, ct_ref, cx_scr, ch_scr) = refs
  else:
    (x_ref, y_ref, ga_ref, prev_ref, h0_ref, cw_ref, cb_ref, wa_ref, ba_ref, wx_ref, bx_ref,
     ap_ref, hl_ref) = refs
  rows = x_ref.shape[0]
  nt = rows // 8
  x = x_ref[...]
  x3 = x.reshape(nt, 8, D_MODEL)
  if chain:
    c = pl.program_id(1)

    @pl.when(c == 0)
    def _():
      cx_scr[...] = jnp.zeros_like(cx_scr)
      ch_scr[...] = jnp.zeros_like(ch_scr)

    prev3 = jnp.concatenate([cx_scr[...].reshape(1, 8, D_MODEL), x3[:nt - 1]], axis=0)
    cx_scr[...] = x_ref[rows - 8:rows, :]
    ct_ref[0] = x_ref[rows - 8:rows, :]
  else:
    prev3 = prev_ref[...].reshape(nt, 8, D_MODEL)

  sub = _iota((nt, 8, D_MODEL), 1)
  u = cb_ref[...] + cw_ref[3:4, :] * x
  for j in (1, 2, 3):
    xs = jnp.where(sub >= j, pltpu.roll(x3, j, 1), pltpu.roll(prev3, j, 1))
    u = u + cw_ref[3 - j:4 - j, :] * xs.reshape(rows, D_MODEL)

  gr, gi = [], []
  for g in range(N_GROUPS):
    ug = u[:, _gsl(g)].astype(BF16)
    gr.append(_dot(ug, wa_ref[g]))
    gi.append(_dot(ug, wx_ref[g]))
  nsplit = 4 if chain else 1
  prow = rows // nsplit
  pt = prow // 8
  sub = _iota((pt, 8, LANES), 1)
  row0 = _iota((prow, LANES), 0) == 0
  blocks = []
  for j in range(D_MODEL // LANES):
    cs = slice(j * LANES, (j + 1) * LANES)
    gs = slice((j % 2) * LANES, (j % 2 + 1) * LANES)
    hc = ch_scr[7:8, cs] if chain else None
    parts = []
    for part in range(nsplit):
      yield
      rs = slice(part * prow, (part + 1) * prow)
      gate_r = jax.nn.sigmoid(gr[j // 2][rs, gs] + ba_ref[:, cs])
      gate_i = jax.nn.sigmoid(gi[j // 2][rs, gs] + bx_ref[:, cs])
      log_a = (-LRU_C * gate_r) * _softplus(ap_ref[:, cs])
      a = jnp.exp(log_a)
      mult = jnp.sqrt(_neg_expm1_2x(log_a, a))
      if chain and part == 0:
        mult = jnp.where(jnp.logical_and(row0, c == 0), 1.0, mult)
      xin = u[rs, cs] * gate_i * mult

      a3 = a.reshape(pt, 8, LANES)
      b3 = xin.reshape(pt, 8, LANES)
      for s in (1, 2, 4):
        m = sub >= s
        b3 = jnp.where(m, a3 * pltpu.roll(b3, s, 1) + b3, b3)
        a3 = jnp.where(m, a3 * pltpu.roll(a3, s, 1), a3)
      if chain:
        hs = []
        for i in range(pt):
          hi = a3[i] * hc + b3[i]
          hs.append(hi)
          hc = hi[7:8, :]
        h = jnp.concatenate(hs, axis=0)
        if part == nsplit - 1:
          ch_scr[:, cs] = hs[-1]
          hl_ref[0, :, cs] = hc
      else:
        h3 = a3 * h0_ref[:, :, cs] + b3
        hl_ref[:, :, cs] = h3[:, 7:8, :]
        h = h3.reshape(prow, LANES)
      parts.append(jax.nn.sigmoid(ga_ref[rs, cs]) * (h * _gelu_tanh(y_ref[rs, cs])))
    blocks.append(_cat(parts, 0))
  return jnp.concatenate(blocks, axis=1)


def _lru_step(xy, gate_a, prev, h0, prm, l, *, nseq, tlen, nb=16):
  rows = nb * tlen
  row = lambda col: pl.BlockSpec((rows, D_MODEL), lambda b: (b, col))
  seq = pl.BlockSpec((nb, 1, D_MODEL), lambda b: (b, 0, 0))
  vec = pl.BlockSpec((None, 1, D_MODEL), lambda b: (l, 0, 0))
  gates_w = pl.BlockSpec((None, N_GROUPS, GROUP, GROUP), lambda b: (l, 0, 0, 0))
  return pl.pallas_call(
      _lru_step_kernel,
      grid=(nseq // nb,),
      in_specs=[row(0), row(1), row(0), row(0), seq,
                pl.BlockSpec((None, 4, D_MODEL), lambda b: (l, 0, 0)), vec, gates_w, vec, gates_w, vec, vec],
      out_specs=[row(0), seq],
      out_shape=[jax.ShapeDtypeStruct((nseq * tlen, D_MODEL), F32), jax.ShapeDtypeStruct((nseq, 1, D_MODEL), F32)],
      compiler_params=_params(("arbitrary",)),
      name="lru_step",
  )(xy, xy, gate_a, prev, h0, prm["conv_w"], prm["conv_b"], prm["wa_bd"], prm["lru_ba"], prm["wx_bd"],
    prm["lru_bx"], prm["lru_a_param"])


def _wkv_masks(tlen):
  lt = tlen.bit_length() - 1
  ri = _iota((CHUNK, GROUP), 0)
  cm = _iota((CHUNK, GROUP), 1) & (CHUNK - 1)
  same = (ri >> lt) == (cm >> lt)
  lane = _iota((CHUNK, LANES), 1)
  half = [jnp.where(lane < HEAD, 1.0, 0.0), jnp.where(lane >= HEAD, 1.0, 0.0)]
  return dict(
      strict=jnp.logical_and(same, cm < ri),
      incl=jnp.logical_and(same, cm <= ri),
      eye=jnp.where(cm == ri, 1.0, 0.0),
      half=half,
      half_b=[h.astype(BF16) for h in half],
      zero_b=jnp.zeros((CHUNK, LANES), BF16),
      ones_bd=_ones_bd(),
      colseq=(_iota((GROUP, 2 * CHUNK), 1) & (CHUNK - 1)) >> lt,
      lt=lt,
  )


def _bd_from_tiles(tiles, mk):
  z = mk["zero_b"]
  return jnp.concatenate([
      jnp.concatenate([tiles[0], z], axis=1), jnp.concatenate([tiles[1], z], axis=1),
      jnp.concatenate([z, tiles[2]], axis=1), jnp.concatenate([z, tiles[3]], axis=1)], axis=0)


def _bd(x, mk):
  xb = x.astype(BF16)
  return _bd_from_tiles([xb[:, (i // 2) * LANES:(i // 2 + 1) * LANES] * mk["half_b"][i % 2] for i in range(4)],
                        mk)


def _wkv_kernel(chain, tlen, cps, *refs):
  if chain:
    ops_ref, dt_ref, gg_ref, gbon_ref = refs[:4]
    lru_in = refs[4:14]
    gnw_ref, gnb_ref, out_ref, sout_ref, hl_ref, ct_ref, s_scr, cx_scr, ch_scr = refs[14:]
    lru = _Staged(_lru_body(True, *lru_in, hl_ref, ct_ref, cx_scr, ch_scr))
    lru.tick()
    tick = lru.tick
  else:
    tick = lambda: None
    ops_ref, dt_ref, pa_ref, gg_ref, gbon_ref, s0_ref, gnw_ref, gnb_ref, out_ref, sout_ref = refs
    pa = pa_ref[...]
  nsq = CHUNK // tlen
  mk = _wkv_masks(tlen)
  groups = range(N_GROUPS)
  units = [(ci, gi) for ci in range(cps) for gi in groups]
  cut = lambda ref, un: ref[un[0] * CHUNK:(un[0] + 1) * CHUNK, _gsl(un[1])]
  op = lambda idx, un: ops_ref[idx, un[0] * CHUNK:(un[0] + 1) * CHUNK, _gsl(un[1])]
  tile_of = lambda i: slice((i // 2) * LANES, (i // 2 + 1) * LANES)
  rblk = lambda i: slice(i * HEAD, (i + 1) * HEAD)

  if chain:
    c = pl.program_id(1)

    @pl.when(c == 0)
    def _():
      s_scr[...] = jnp.zeros_like(s_scr)

  lhs, n_ab, a_ak, a_rbk = {}, {}, {}, {}
  for idx, un in enumerate(units):
    lhs[un] = jnp.concatenate([op(0, un), op(1, un)], axis=0)
    gb = _dot(lhs[un], _bd(op(2, un), mk), trans_b=True)
    gk = _dot(lhs[un], _bd(op(3, un), mk), trans_b=True)
    n_ab[un] = jnp.where(mk["strict"], gb[:CHUNK], 0.0)
    a_ak[un] = jnp.where(mk["strict"], gk[:CHUNK], 0.0)
    a_rbk[un] = jnp.concatenate([jnp.where(mk["incl"], gb[CHUNK:], 0.0),
                                 jnp.where(mk["incl"], gk[CHUNK:], 0.0)], axis=1)
    if idx % 2:
      tick()

  t = {un: mk["eye"] + n_ab[un] for un in units}
  q = {un: _dot(n_ab[un], _bd(n_ab[un], mk)) for un in units}
  tick()
  levels = mk["lt"]
  for lvl in range(1, levels):
    for idx, un in enumerate(units):
      bq = _bd(q[un], mk)
      if lvl + 1 < levels:
        tq = _dot(jnp.concatenate([t[un], q[un]], axis=0), bq)
        t[un] = t[un] + tq[:CHUNK]
        q[un] = tq[CHUNK:]
      else:
        t[un] = t[un] + _dot(t[un], bq)
      if idx % 2:
        tick()

  bdv, a_v = {}, {}
  for idx, un in enumerate(units):
    bdv[un] = _bd(op(6, un), mk)
    a_v[un] = _dot(a_ak[un], bdv[un])
    if idx % 2:
      tick()
  lane_half = lambda i: slice((i % 2) * HEAD, (i % 2 + 1) * HEAD)
  o, gn = {}, {}

  def group_norm_staged(og, gi):
    o_mu = _dot(og, mk["ones_bd"]) * (1.0 / HEAD)
    yield
    oc = og - o_mu
    o_var = _dot(oc * oc, mk["ones_bd"]) * (1.0 / HEAD)
    return oc * lax.rsqrt(o_var + GN_EPS) * gnw_ref[:, _gsl(gi)] + gnb_ref[:, _gsl(gi)]

  def group_norm(og, gi):
    return _Staged(group_norm_staged(og, gi)).finish()

  if chain:
    xt, dcol = {}, {}
    for un in units:
      dt = cut(dt_ref, un)
      xpt = jnp.concatenate([op(4, un).astype(F32), op(5, un).astype(F32), dt, dt], axis=0).T
      xt[un] = xpt[:, :2 * CHUNK]
      dcol[un] = xpt[:, 2 * CHUNK:]
    p_cur = {gi: [s_scr[gi, i] for i in range(4)] for gi in groups}
    for ci in range(cps):
      a_p, r_p, u = {}, {}, {}
      for gi in groups:
        lp = _dot(lhs[(ci, gi)], _bd_from_tiles([tl.astype(BF16) for tl in p_cur[gi]], mk))
        a_p[gi], r_p[gi] = lp[:CHUNK], lp[CHUNK:]
      tick()
      if ci > 0:
        pend = [_Staged(group_norm_staged(o[(ci - 1, gi)], gi)) for gi in groups]
        for st in pend:
          st.tick()
      for gi in groups:
        u[gi] = _dot(t[(ci, gi)], _bd(a_p[gi] + a_v[(ci, gi)], mk))
      if ci > 0:
        for gi, st in zip(groups, pend):
          gn[(ci - 1, gi)] = st.finish()
      tick()
      for gi in groups:
        un = (ci, gi)
        y = jnp.concatenate([u[gi].astype(BF16), op(6, un)], axis=0)
        upd = _dot(xt[un], y)
        p_cur[gi] = [(p_cur[gi][i] * dcol[un][rblk(i)] + upd[rblk(i), tile_of(i)]) * mk["half"][i % 2]
                     for i in range(4)]
      tick()
      for gi in groups:
        un = (ci, gi)
        o[un] = r_p[gi] + _dot(a_rbk[un], jnp.concatenate([_bd(u[gi], mk), bdv[un]], axis=0))
      tick()
    for gi in groups:
      gn[(cps - 1, gi)] = group_norm(o[(cps - 1, gi)], gi)
    for gi in groups:
      for i in range(4):
        s_scr[gi, i] = p_cur[gi][i]

    @pl.when(c == pl.num_programs(1) - 1)
    def _():
      for gi in groups:
        for i in range(4):
          sout_ref[0, 4 * gi + i] = s_scr[gi, i][:, lane_half(i)].T
  else:
    zero = jnp.zeros((HEAD, HEAD), F32)
    tiles, a_p, r_p, u = {}, {}, {}, {}
    for un in units:
      ci, gi = un
      tiles[un], aps, rps = [], [], []
      for j in range(nsq):
        st = [s0_ref[ci * nsq + j, 4 * gi + i] for i in range(4)]
        tiles[un].append([jnp.concatenate([st[i], zero] if i % 2 == 0 else [zero, st[i]], axis=1)
                          for i in range(4)])
        lj = jnp.concatenate([lhs[un][j * tlen:(j + 1) * tlen],
                              lhs[un][CHUNK + j * tlen:CHUNK + (j + 1) * tlen]], axis=0)
        lpj = _dot(lj, _bd_from_tiles([tl.astype(BF16) for tl in tiles[un][j]], mk), trans_b=True)
        aps.append(lpj[:tlen])
        rps.append(lpj[tlen:])
      a_p[un] = jnp.concatenate(aps, axis=0)
      r_p[un] = jnp.concatenate(rps, axis=0)
    for un in units:
      u[un] = _dot(t[un], _bd(a_p[un] + a_v[un], mk))
    for un in units:
      o[un] = r_p[un] + _dot(a_rbk[un], jnp.concatenate([_bd(u[un], mk), bdv[un]], axis=0))
    for un in units:
      ci, gi = un
      yt = jnp.concatenate([u[un], op(6, un).astype(F32)], axis=0).T
      x = jnp.concatenate([op(4, un), op(5, un)], axis=0)
      dt = cut(dt_ref, un)
      for j in range(nsq):
        upd = _dot(jnp.where(mk["colseq"] == j, yt, 0.0), x)
        for i in range(4):
          new = tiles[un][j][i] * dt[j * tlen:j * tlen + 1, tile_of(i)] + upd[rblk(i), tile_of(i)]
          sout_ref[ci * nsq + j, 4 * gi + i] = new[:, lane_half(i)]

  if chain:
    outs = [_cat([gn[(ci, gi)] for ci in range(cps)], 0) for gi in groups]
    pa = lru.finish()
  else:
    outs = [group_norm(_cat([o[(ci, gi)] for ci in range(cps)], 0), gi) for gi in groups]
  out_ref[...] = pa + gbon_ref[...] + gg_ref[...] * jnp.concatenate(outs, axis=1)


N_HEADS = 4 * N_GROUPS


def _wkv_chain(ops, dtot, gg, gbon, xy, gate_a, prm, l, *, nseq, tlen, cps=2):
  rows = cps * CHUNK
  nstep = tlen // rows
  rmap = lambda b, c: (b * nstep + c, 0)
  row = pl.BlockSpec((rows, D_MODEL), rmap)
  vec = pl.BlockSpec((None, 1, D_MODEL), lambda b, c: (l, 0, 0))
  gates_w = pl.BlockSpec((None, N_GROUPS, GROUP, GROUP), lambda b, c: (l, 0, 0, 0))
  per_seq = lambda r, w: pl.BlockSpec((1, r, w), lambda b, c: (b, 0, 0))
  in_specs = [pl.BlockSpec((N_WKV_OPS, rows, D_MODEL), lambda b, c: (0, b * nstep + c, 0)), row, row, row,
              row, pl.BlockSpec((rows, D_MODEL), lambda b, c: (b * nstep + c, 1)), row,
              pl.BlockSpec((None, 4, D_MODEL), lambda b, c: (l, 0, 0)), vec, gates_w, vec, gates_w, vec, vec,
              vec, vec]
  args = [ops, dtot, gg, gbon, xy, xy, gate_a,
          prm["conv_w"], prm["conv_b"], prm["wa_bd"], prm["lru_ba"], prm["wx_bd"], prm["lru_bx"],
          prm["lru_a_param"], prm["gn_w"], prm["gn_b"]]
  return pl.pallas_call(
      functools.partial(_wkv_kernel, True, CHUNK, cps),
      grid=(nseq, nstep), in_specs=in_specs,
      out_specs=[row, pl.BlockSpec((1, N_HEADS, HEAD, HEAD), lambda b, c: (b, 0, 0, 0)),
                 per_seq(1, D_MODEL), per_seq(8, D_MODEL)],
      out_shape=[jax.ShapeDtypeStruct((nseq * tlen, D_MODEL), F32),
                 jax.ShapeDtypeStruct((nseq, N_HEADS, HEAD, HEAD), F32),
                 jax.ShapeDtypeStruct((nseq, 1, D_MODEL), F32), jax.ShapeDtypeStruct((nseq, 8, D_MODEL), F32)],
      scratch_shapes=[pltpu.VMEM((N_GROUPS, 4, HEAD, LANES), F32), pltpu.VMEM((8, D_MODEL), F32),
                      pltpu.VMEM((8, D_MODEL), F32)],
      compiler_params=_params(("arbitrary", "arbitrary")),
      name="wkv_chain",
  )(*args)


def _wkv_step(ops, dtot, pa, gg, gbon, states, prm, l, *, nseq, tlen, cps=2):
  rows = cps * CHUNK
  sblk = rows // tlen
  row = pl.BlockSpec((rows, D_MODEL), lambda b, c: (b, 0))
  vec = pl.BlockSpec((None, 1, D_MODEL), lambda b, c: (l, 0, 0))
  sspec = pl.BlockSpec((None, sblk, N_HEADS, HEAD, HEAD), lambda b, c: (l, b, 0, 0, 0))
  in_specs = [pl.BlockSpec((N_WKV_OPS, rows, D_MODEL), lambda b, c: (0, b, 0)), row, row, row, row, sspec,
              vec, vec]
  return pl.pallas_call(
      functools.partial(_wkv_kernel, False, tlen, cps),
      grid=(nseq // sblk, 1), in_specs=in_specs,
      out_specs=[row, sspec],
      out_shape=[jax.ShapeDtypeStruct((nseq * tlen, D_MODEL), F32),
                 jax.ShapeDtypeStruct(states.shape, F32)],
      input_output_aliases={5: 1},
      compiler_params=_params(("arbitrary", "arbitrary")),
      name="wkv_step",
  )(ops, dtot, pa, gg, gbon, states, prm["gn_w"], prm["gn_b"])


def _mlp_kernel(x_ref, m_ref, wo_ref, g1_ref, b1_ref, w1_ref, w2_ref, g2_ref, b2_ref,
                o_ref, x1_scr, x1b_scr, acc_scr):
  j = pl.program_id(1)
  last = pl.num_programs(1) - 1
  sub = 256
  nb = x_ref.shape[0] // sub
  rb = lambda b: slice(b * sub, (b + 1) * sub)

  def outproj(b):
    return DN_ALPHA * x_ref[rb(b), :] + _dot(m_ref[rb(b), :], wo_ref[...])

  def ln1(b, y):
    x1 = _layer_norm(y, g1_ref[...], b1_ref[...])
    x1_scr[rb(b), :] = x1
    x1b_scr[rb(b), :] = x1.astype(BF16)

  def ff(b, first):
    hid = jnp.square(jnp.maximum(_dot(x1b_scr[rb(b), :], w1_ref[...]), 0.0))
    part = _dot(hid, w2_ref[...])
    acc_scr[rb(b), :] = part if first else acc_scr[rb(b), :] + part

  def ln2(b):
    o_ref[rb(b), :] = _layer_norm(DN_ALPHA * x1_scr[rb(b), :] + acc_scr[rb(b), :], g2_ref[...], b2_ref[...])

  @pl.when(j == 0)
  def _():
    y = outproj(0)
    for b in range(nb):
      y_next = outproj(b + 1) if b + 1 < nb else None
      ln1(b, y)
      if b > 0:
        ff(b - 1, True)
      y = y_next
    ff(nb - 1, True)

  @pl.when(jnp.logical_and(j > 0, j < last))
  def _():
    for b in range(nb):
      ff(b, False)

  @pl.when(j == last)
  def _():
    for b in range(nb):
      ff(b, False)
      if b > 0:
        ln2(b - 1)
    ln2(nb - 1)


def _mlp(x, merged, prm, l, tm=512, tf=1024):
  n = x.shape[0]
  vec = pl.BlockSpec((None, 1, D_MODEL), lambda i, j: (l, 0, 0))
  row = pl.BlockSpec((tm, D_MODEL), lambda i, j: (i, 0))
  return pl.pallas_call(
      _mlp_kernel,
      grid=(n // tm, D_FF // tf),
      in_specs=[
          row, row,
          pl.BlockSpec((None, D_MODEL, D_MODEL), lambda i, j: (l, 0, 0)),
          vec, vec,
          pl.BlockSpec((None, D_MODEL, tf), lambda i, j: (l, 0, j)),
          pl.BlockSpec((None, tf, D_MODEL), lambda i, j: (l, j, 0)),
          vec, vec,
      ],
      out_specs=row,
      out_shape=jax.ShapeDtypeStruct((n, D_MODEL), F32),
      scratch_shapes=[pltpu.VMEM((tm, D_MODEL), F32), pltpu.VMEM((tm, D_MODEL), BF16),
                      pltpu.VMEM((tm, D_MODEL), F32)],
      compiler_params=_params(("arbitrary", "arbitrary")),
      name="outproj_mlp",
  )(x, merged, prm["w_out"], prm["ln1_g"], prm["ln1_b"], prm["mlp_w1"], prm["mlp_w2"],
    prm["ln2_g"], prm["ln2_b"])


def _block_diag4(w):
  depth = w.shape[0]
  w5 = w.reshape(depth, N_GROUPS, 4, HEAD, HEAD)
  eye = jnp.eye(4, dtype=w.dtype)
  return jnp.einsum("lgaij,ab->lgaibj", w5, eye).reshape(depth, N_GROUPS, GROUP, GROUP)


def kernel(x_prompt, x_sample, state_conv, state_lru, state_shift, state_wkv, w_in, conv_w, conv_b, lru_wa,
           lru_ba, lru_wx, lru_bx, lru_a_param, shift_mu, decay_up, w0, aaa_up, a0, gate_up, k_k, k_a, r_k,
           gn_w, gn_b, w_out, ln1_g, ln1_b, mlp_w1, mlp_w2, ln2_g, ln2_b):
  bp, tp, _ = x_prompt.shape
  bs, ts, _ = x_sample.shape
  depth = w_in.shape[0]

  row = lambda p: p.reshape(depth, 1, -1)
  w_in_b = w_in.astype(BF16)
  zero = jnp.zeros((depth, 64, D_MODEL), F32)
  lora_w = jnp.concatenate([jnp.concatenate([decay_up, zero], axis=2),
                            jnp.concatenate([zero, aaa_up], axis=2)], axis=1).astype(BF16)
  prm = dict(
      conv_w=conv_w, conv_b=row(conv_b),
      wa_bd=_block_diag4(lru_wa).astype(BF16), wx_bd=_block_diag4(lru_wx).astype(BF16),
      lru_ba=row(lru_ba), lru_bx=row(lru_bx), lru_a_param=row(lru_a_param),
      shift_mu=row(shift_mu), lora_w=lora_w, w0=row(w0), a0=row(a0), gate_up=gate_up.astype(BF16),
      k_k=row(k_k), k_a=row(k_a), r_k=row(r_k), gn_w=row(gn_w), gn_b=row(gn_b),
      w_out=w_out.astype(BF16), ln1_g=row(ln1_g), ln1_b=row(ln1_b),
      mlp_w1=mlp_w1.astype(BF16), mlp_w2=mlp_w2.astype(BF16), ln2_g=row(ln2_g), ln2_b=row(ln2_b),
  )

  xp = x_prompt.reshape(bp * tp, D_MODEL)
  xs = x_sample.reshape(bs * ts, D_MODEL)
  conv_p, lru_p, shift_p, wkv_p = [], [], [], []
  conv_s, lru_s, shift_s = [], [], []
  wkv_s = state_wkv
  for l in range(depth):
    xy, ga, ops, dtot, gg, gbon, stail = _front(xp, w_in_b, prm, l, chain=True, nseq=bp, tlen=tp)
    mg, z, hl, ctail = _wkv_chain(ops, dtot, gg, gbon, xy, ga, prm, l, nseq=bp, tlen=tp)
    xp = _mlp(xp, mg, prm, l, tm=1024)
    conv_p.append(ctail[:, 5:, :])
    lru_p.append(hl.reshape(bp, D_MODEL))
    shift_p.append(stail[:, 7, :])
    wkv_p.append(z)

    xy, ga, ops, dtot, gg, gbon, sh = _front(xs, w_in_b, prm, l, chain=False, nseq=bs, tlen=ts, tm=128,
                                             sprev=jnp.repeat(state_shift[l], ts, axis=0))
    prev = jnp.pad(state_conv[l], ((0, 0), (5, 0), (0, 0))).reshape(bs * 8, D_MODEL)
    pa, hl = _lru_step(xy, ga, prev, state_lru[l].reshape(bs, 1, D_MODEL), prm, l, nseq=bs, tlen=ts)
    mg, wkv_s = _wkv_step(ops, dtot, pa, gg, gbon, wkv_s, prm, l, nseq=bs, tlen=ts)
    xs = _mlp(xs, mg, prm, l, tm=1024)
    conv_s.append(xy.reshape(bs, ts, XY_WIDTH)[:, ts - 3:, :D_MODEL])
    lru_s.append(hl.reshape(bs, D_MODEL))
    shift_s.append(sh.reshape(bs, ts, SHIFT_WIDTH)[:, ts - 1])

  return (xp.reshape(bp, tp, D_MODEL), xs.reshape(bs, ts, D_MODEL),
          jnp.stack(conv_p), jnp.stack(lru_p), jnp.stack(shift_p), jnp.stack(wkv_p),
          jnp.stack(conv_s), jnp.stack(lru_s), jnp.stack(shift_s), wkv_s)
```

```python
import functools
import math

import jax
import jax.numpy as jnp
from jax import lax
from jax.experimental import pallas as pl
from jax.experimental.pallas import tpu as pltpu

F32 = jnp.float32
BF16 = jnp.bfloat16

D_MODEL = 1024
DEPTH = 4
LRU_C = 8.0
HEAD = 64
GROUP = 256
LANES = 128
N_GROUPS = D_MODEL // GROUP
SHIFT_WIDTH = 3 * D_MODEL + 64 + 64 + 128
XY_WIDTH = 2 * D_MODEL
IN_COLS = 2 * XY_WIDTH + SHIFT_WIDTH
D_FF = 4 * D_MODEL
DN_ALPHA = (2 * DEPTH) ** 0.25
LN_EPS = 1e-5
GN_EPS = 64e-5
CHUNK = 64
VMEM_LIMIT = 56 * 1024 * 1024
NP_CUM = 2
N_WKV_OPS = 7


def _split(x, n):
  pieces = []
  rem = x
  for i in range(n):
    p = rem.astype(BF16)
    pieces.append(p)
    if i + 1 < n:
      rem = rem - p.astype(F32)
  return pieces


def _dot(a, b, trans_b=False):
  dims = (((1,), (1,)), ((), ())) if trans_b else (((1,), (0,)), ((), ()))
  return lax.dot_general(a.astype(BF16), b.astype(BF16), dims, preferred_element_type=F32)


def _softplus(x):
  return jnp.maximum(x, 0.0) + jnp.log1p(jnp.exp(-jnp.abs(x)))


def _neg_expm1_2x(x, ex):
  return -jnp.tanh(x) * (ex * ex + 1.0)


def _gelu_tanh(x):
  c = math.sqrt(2.0 / math.pi)
  return x * (0.5 * (1.0 + jnp.tanh(c * (x + 0.044715 * (x * x * x)))))


def _layer_norm(x, g, b):
  mu = jnp.mean(x, axis=-1, keepdims=True)
  xc = x - mu
  var = jnp.mean(xc * xc, axis=-1, keepdims=True)
  return xc * lax.rsqrt(var + LN_EPS) * g + b


def _iota(shape, dim):
  return lax.broadcasted_iota(jnp.int32, shape, dim)


def _cat(xs, axis):
  return xs[0] if len(xs) == 1 else jnp.concatenate(xs, axis=axis)


def _params(sem):
  return pltpu.CompilerParams(dimension_semantics=sem, vmem_limit_bytes=VMEM_LIMIT)


def _ones_bd():
  same = (_iota((GROUP, GROUP), 0) >> 6) == (_iota((GROUP, GROUP), 1) >> 6)
  return jnp.where(same, 1.0, 0.0).astype(BF16)


def _gsl(gi):
  return slice(gi * GROUP, (gi + 1) * GROUP)


def _front_dots(x_ref, w_ref, xy_ref, ga_ref):
  xb = x_ref[...].astype(BF16)
  sh = _dot(xb, w_ref[:, XY_WIDTH:XY_WIDTH + SHIFT_WIDTH])
  xy_ref[...] = _dot(xb, w_ref[:, 0:XY_WIDTH])
  gt = _dot(xb, w_ref[:, XY_WIDTH + SHIFT_WIDTH:IN_COLS])
  ga_ref[...] = gt[:, :D_MODEL]
  return sh, gt[:, D_MODEL:]


def _front_prep(sh, gate_b, prev, tlen, prm_refs, out_refs, r0=0):
  mu_ref, lw_ref, w0_ref, a0_ref, gup_ref, kk_ref, ka_ref, rk_ref = prm_refs
  ops_ref, dt_ref, gg_ref, gbon_ref = out_refs
  rows = sh.shape[0]
  rsl = slice(r0, r0 + rows)
  nchunk = rows // CHUNK
  lt = tlen.bit_length() - 1
  ones_bd = _ones_bd()
  headsum = lambda z: _cat([_dot(z[:, _gsl(gi)], ones_bd) for gi in range(N_GROUPS)], 1)

  mixed = sh + (prev - sh) * mu_ref[...]
  r = mixed[:, 0:D_MODEL]
  k = mixed[:, D_MODEL:2 * D_MODEL]
  v = mixed[:, 2 * D_MODEL:3 * D_MODEL]
  xl = mixed[:, 3 * D_MODEL:3 * D_MODEL + 128]
  xg = mixed[:, 3 * D_MODEL + 128:SHIFT_WIDTH]
  lane = _iota((rows, 128), 1)
  lo = _dot(jnp.where(lane < 64, jnp.tanh(xl), xl), lw_ref[...])
  g = _dot(jax.nn.sigmoid(xg), gup_ref[...])
  kk = k * kk_ref[...]
  ss = headsum(kk * kk)
  yield

  w_log = -_softplus(-(w0_ref[...] + lo[:, :D_MODEL])) - 0.5
  w = -jnp.exp(w_log)
  a = jax.nn.sigmoid(a0_ref[...] + lo[:, D_MODEL:])
  k_h = k * (1.0 + (a - 1.0) * ka_ref[...])
  rkr = r * k_h * rk_ref[...]
  bonus_sum = headsum(rkr)
  rs = _iota((2 * CHUNK, CHUNK), 0)
  cs = _iota((2 * CHUNK, CHUNK), 1)
  rr = rs & (CHUNK - 1)
  cum_mask = jnp.logical_and((rr >> lt) == (cs >> lt), jnp.logical_or(cs <= rr, rs >= CHUNK))
  cum_mask = jnp.where(cum_mask, 1.0, 0.0).astype(BF16)
  one_seq = tlen == CHUNK
  if one_seq:
    cum_mask = cum_mask[:CHUNK]
  cums, tots = [], []
  for ci in range(nchunk):
    pieces = _split(w[ci * CHUNK:(ci + 1) * CHUNK], NP_CUM)
    cw = _dot(cum_mask, pieces[0])
    for p in pieces[1:]:
      cw = cw + _dot(cum_mask, p)
    cums.append(cw[:CHUNK])
    tots.append(jnp.broadcast_to(cw[CHUNK - 1:CHUNK], (CHUNK, D_MODEL)) if one_seq else cw[CHUNK:])
  yield

  kkn = kk / jnp.maximum(jnp.sqrt(ss), 1e-12)
  bv = kkn * a
  gg = jax.nn.sigmoid(gate_b) * g
  gg_ref[rsl, :] = gg
  gbon_ref[rsl, :] = gg * (bonus_sum * v)
  yield
  cum = _cat(cums, 0)
  tot = _cat(tots, 0)
  e_neg = jnp.exp(-cum)
  e_rem = jnp.exp(tot - cum)
  ops_ref[0, rsl, :] = (-kkn * jnp.exp(cum - w)).astype(BF16)
  ops_ref[1, rsl, :] = (r * jnp.exp(cum)).astype(BF16)
  ops_ref[2, rsl, :] = (bv * e_neg).astype(BF16)
  ops_ref[3, rsl, :] = (k_h * e_neg).astype(BF16)
  yield
  ops_ref[4, rsl, :] = (bv * e_rem).astype(BF16)
  ops_ref[5, rsl, :] = (k_h * e_rem).astype(BF16)
  ops_ref[6, rsl, :] = v.astype(BF16)
  dt_ref[rsl, :] = jnp.exp(tot)


def _front_prep_fine(mix, gate_b, prm_refs, out_refs, r0):
  mu_ref, lw_ref, w0_ref, a0_ref, gup_ref, kk_ref, ka_ref, rk_ref = prm_refs
  ops_ref, dt_ref, gg_ref, gbon_ref = out_refs
  rsl = slice(r0, r0 + CHUNK)
  ones_bd = _ones_bd()
  cum_mask = jnp.where(_iota((CHUNK, CHUNK), 1) <= _iota((CHUNK, CHUNK), 0), 1.0, 0.0).astype(BF16)
  lane = _iota((CHUNK, LANES), 1)
  col = lambda base, gi: slice(base + gi * GROUP, base + (gi + 1) * GROUP)

  xl = mix(slice(3 * D_MODEL, 3 * D_MODEL + LANES))
  xg = mix(slice(3 * D_MODEL + LANES, SHIFT_WIDTH))
  lo = _dot(jnp.where(lane < 64, jnp.tanh(xl), xl), lw_ref[...])
  g = _dot(jax.nn.sigmoid(xg), gup_ref[...])
  yield
  st = {}
  for gi in range(N_GROUPS):
    k = mix(col(D_MODEL, gi))
    kk = k * kk_ref[:, _gsl(gi)]
    st[gi] = dict(k=k, kk=kk, ss=_dot(kk * kk, ones_bd))
    yield
  for gi in range(N_GROUPS):
    cs, d = _gsl(gi), st[gi]
    w = -jnp.exp(-_softplus(-(w0_ref[:, cs] + lo[:, cs])) - 0.5)
    a = jax.nn.sigmoid(a0_ref[:, cs] + lo[:, col(D_MODEL, gi)])
    k_h = d["k"] * (1.0 + (a - 1.0) * ka_ref[:, cs])
    r = mix(col(0, gi))
    bsum = _dot(r * k_h * rk_ref[:, cs], ones_bd)
    pieces = _split(w, NP_CUM)
    cum = _dot(cum_mask, pieces[0])
    for p in pieces[1:]:
      cum = cum + _dot(cum_mask, p)
    d.update(w=w, a=a, k_h=k_h, r=r, bsum=bsum, cum=cum)
    yield
  for gi in range(N_GROUPS):
    cs, d = _gsl(gi), st[gi]
    v = mix(col(2 * D_MODEL, gi))
    kkn = d["kk"] / jnp.maximum(jnp.sqrt(d["ss"]), 1e-12)
    bv = kkn * d["a"]
    gg = jax.nn.sigmoid(gate_b(cs)) * g[:, cs]
    gg_ref[rsl, cs] = gg
    gbon_ref[rsl, cs] = gg * (d["bsum"] * v)
    cum = d["cum"]
    tot = jnp.broadcast_to(cum[CHUNK - 1:CHUNK], cum.shape)
    e_neg = jnp.exp(-cum)
    e_rem = jnp.exp(tot - cum)
    ops_ref[0, rsl, cs] = (-kkn * jnp.exp(cum - d["w"])).astype(BF16)
    ops_ref[1, rsl, cs] = (d["r"] * jnp.exp(cum)).astype(BF16)
    ops_ref[2, rsl, cs] = (bv * e_neg).astype(BF16)
    ops_ref[3, rsl, cs] = (d["k_h"] * e_neg).astype(BF16)
    ops_ref[4, rsl, cs] = (bv * e_rem).astype(BF16)
    ops_ref[5, rsl, cs] = (d["k_h"] * e_rem).astype(BF16)
    ops_ref[6, rsl, cs] = v.astype(BF16)
    dt_ref[rsl, cs] = jnp.exp(tot)
    yield


def _front_step_kernel(tlen, x_ref, w_ref, sp_ref, *refs):
  prm_refs, (xy_ref, ga_ref), out_refs, sh_ref = refs[:8], refs[8:10], refs[10:14], refs[14]
  sh, gate_b = _front_dots(x_ref, w_ref, xy_ref, ga_ref)
  sh_ref[...] = sh
  row = _iota(sh.shape, 0)
  prev = jnp.where((row & (tlen - 1)) == 0, sp_ref[...], pltpu.roll(sh, 1, 0))
  for _ in _front_prep(sh, gate_b, prev, tlen, prm_refs, out_refs):
    pass


def _front_chain_kernel(nstep, ntiles, x_ref, w_ref, *refs):
  prm_refs, (xy_ref, ga_ref), out_refs = refs[:8], refs[8:10], refs[10:14]
  st_ref, sh_a, sh_b, gb_a, gb_b, carry_scr = refs[14:]
  s = pl.program_id(0)
  rows = x_ref.shape[0]
  bufs = ((sh_a, gb_a), (sh_b, gb_b))

  nchunk = rows // CHUNK
  step_w = GROUP
  pieces = [("xy", c, c + step_w) for c in range(0, XY_WIDTH, step_w)]
  pieces += [("sh", c, min(c + step_w, SHIFT_WIDTH)) for c in range(0, SHIFT_WIDTH, step_w)]
  pieces += [("ga", c, c + step_w) for c in range(0, D_MODEL, step_w)]
  pieces += [("gb", c, c + step_w) for c in range(0, D_MODEL, step_w)]
  w_base = {"xy": 0, "sh": XY_WIDTH, "ga": XY_WIDTH + SHIFT_WIDTH, "gb": XY_WIDTH + SHIFT_WIDTH + D_MODEL}

  def dot_piece(parity, piece):
    sh_scr, gb_scr = bufs[parity]
    dst, c0, c1 = piece
    res = _dot(x_ref[...].astype(BF16), w_ref[:, w_base[dst] + c0:w_base[dst] + c1])
    {"xy": xy_ref, "sh": sh_scr, "ga": ga_ref, "gb": gb_scr}[dst][:, c0:c1] = res

  def prep_piece(parity, ci):
    sh_scr, gb_scr = bufs[parity]
    r0 = ci * CHUNK
    row0 = _iota((CHUNK, LANES), 0) == 0
    seq_start = ((s - 1) % nstep) == 0

    def mix(cs):
      sh = sh_scr[r0:r0 + CHUNK, cs]
      first = jnp.where(seq_start, 0.0, carry_scr[7:8, cs]) if ci == 0 else sh_scr[r0 - 1:r0, cs]
      width = sh.shape[1]
      prev = jnp.where(jnp.concatenate([row0] * (width // LANES), axis=1), first, pltpu.roll(sh, 1, 0))
      return sh + (prev - sh) * mu_ref[:, cs]

    return _front_prep_fine(mix, lambda cs: gb_scr[r0:r0 + CHUNK, cs], prm_refs, out_refs, r0)

  def prep_all(parity, between=lambda done, total: None):
    total = nchunk * (1 + 3 * N_GROUPS)
    done = 0
    for ci in range(nchunk):
      for _ in prep_piece(parity, ci):
        done += 1
        between(done, total)
    tail = bufs[parity][0][rows - 8:rows, :]
    carry_scr[...] = tail
    st_ref[0] = tail

  mu_ref = prm_refs[0]

  @pl.when(s == 0)
  def _():
    for piece in pieces:
      dot_piece(0, piece)

  for parity in (0, 1):
    @pl.when(jnp.logical_and(jnp.logical_and(s > 0, s < ntiles), s % 2 == parity))
    def _(parity=parity):
      todo = list(pieces)

      def between(done, total):
        while todo and (len(pieces) - len(todo)) * total < done * len(pieces):
          dot_piece(parity, todo.pop(0))

      prep_all(1 - parity, between)
      for piece in todo:
        dot_piece(parity, piece)

  @pl.when(s == ntiles)
  def _():
    prep_all((ntiles - 1) % 2)


def _front(x, w_in_b, prm, l, *, chain, nseq, tlen, tm=256, sprev=None):
  n = x.shape[0]
  ntiles = n // tm
  if chain:
    nstep = tlen // tm
    grid = (ntiles + 1,)
    cur = lambda s: jnp.minimum(s, ntiles - 1)
    old = lambda s: jnp.maximum(s - 1, 0)
    body = functools.partial(_front_chain_kernel, nstep, ntiles)
  else:
    grid = (ntiles,)
    cur = old = lambda s: s
    body = functools.partial(_front_step_kernel, tlen)
  row = lambda width, which: pl.BlockSpec((tm, width), lambda s: (which(s), 0))
  vec = pl.BlockSpec((None, 1, D_MODEL), lambda s: (l, 0, 0))
  in_specs = [row(D_MODEL, cur), pl.BlockSpec((None, D_MODEL, IN_COLS), lambda s: (l, 0, 0),
                                              pipeline_mode=pl.Buffered(1))]
  args = [x, w_in_b]
  if not chain:
    in_specs.append(row(SHIFT_WIDTH, cur))
    args.append(sprev)
  in_specs += [pl.BlockSpec((None, 1, SHIFT_WIDTH), lambda s: (l, 0, 0)),
               pl.BlockSpec((None, 128, 2 * D_MODEL), lambda s: (l, 0, 0)),
               vec, vec,
               pl.BlockSpec((None, 128, D_MODEL), lambda s: (l, 0, 0)),
               vec, vec, vec]
  args += [prm["shift_mu"], prm["lora_w"], prm["w0"], prm["a0"], prm["gate_up"], prm["k_k"], prm["k_a"],
           prm["r_k"]]
  out_specs = [row(XY_WIDTH, cur), row(D_MODEL, cur),
               pl.BlockSpec((N_WKV_OPS, tm, D_MODEL), lambda s: (0, old(s), 0)),
               row(D_MODEL, old), row(D_MODEL, old), row(D_MODEL, old)]
  f32 = lambda width: jax.ShapeDtypeStruct((n, width), F32)
  out_shape = [f32(XY_WIDTH), f32(D_MODEL), jax.ShapeDtypeStruct((N_WKV_OPS, n, D_MODEL), BF16),
               f32(D_MODEL), f32(D_MODEL), f32(D_MODEL)]
  scratch = []
  if chain:
    out_specs.append(pl.BlockSpec((1, 8, SHIFT_WIDTH), lambda s: (old(s) // nstep, 0, 0)))
    out_shape.append(jax.ShapeDtypeStruct((nseq, 8, SHIFT_WIDTH), F32))
    scratch = [pltpu.VMEM((tm, SHIFT_WIDTH), F32), pltpu.VMEM((tm, SHIFT_WIDTH), F32),
               pltpu.VMEM((tm, D_MODEL), F32), pltpu.VMEM((tm, D_MODEL), F32),
               pltpu.VMEM((8, SHIFT_WIDTH), F32)]
  else:
    out_specs.append(row(SHIFT_WIDTH, cur))
    out_shape.append(f32(SHIFT_WIDTH))
  return pl.pallas_call(
      body, grid=grid, in_specs=in_specs, out_specs=out_specs, out_shape=out_shape, scratch_shapes=scratch,
      compiler_params=_params(("arbitrary",)),
      name="front_chain" if chain else "front_step",
  )(*args)


class _Staged:
  def __init__(self, gen):
    self.gen, self.value, self.done = gen, None, False

  def tick(self):
    if not self.done:
      try:
        next(self.gen)
      except StopIteration as stop:
        self.value, self.done = stop.value, True

  def finish(self):
    while not self.done:
      self.tick()
    return self.value


def _lru_step_kernel(*refs):
  n_in = 12
  refs[n_in][...] = _Staged(_lru_body(False, *refs[:n_in], *refs[n_in + 1:])).finish()


def _lru_body(chain, *refs):
  if chain:
    (x_ref, y_ref, ga_ref, cw_ref, cb_ref, wa_ref, ba_ref, wx_ref, bx_ref, ap_ref,
     hl_ref, ct_ref, cx_scr, ch_scr) = refs
  else:
    (x_ref, y_ref, ga_ref, prev_ref, h0_ref, cw_ref, cb_ref, wa_ref, ba_ref, wx_ref, bx_ref,
     ap_ref, hl_ref) = refs
  rows = x_ref.shape[0]
  nt = rows // 8
  x = x_ref[...]
  x3 = x.reshape(nt, 8, D_MODEL)
  if chain:
    c = pl.program_id(1)

    @pl.when(c == 0)
    def _():
      cx_scr[...] = jnp.zeros_like(cx_scr)
      ch_scr[...] = jnp.zeros_like(ch_scr)

    prev3 = jnp.concatenate([cx_scr[...].reshape(1, 8, D_MODEL), x3[:nt - 1]], axis=0)
    cx_scr[...] = x_ref[rows - 8:rows, :]
    ct_ref[0] = x_ref[rows - 8:rows, :]
  else:
    prev3 = prev_ref[...].reshape(nt, 8, D_MODEL)

  sub = _iota((nt, 8, D_MODEL), 1)
  u = cb_ref[...] + cw_ref[3:4, :] * x
  for j in (1, 2, 3):
    xs = pltpu.roll(jnp.where(sub >= 8 - j, prev3, x3), j, 1)
    u = u + cw_ref[3 - j:4 - j, :] * xs.reshape(rows, D_MODEL)

  gr, gi = [], []
  for g in range(N_GROUPS):
    ug = u[:, _gsl(g)].astype(BF16)
    gr.append(_dot(ug, wa_ref[g]))
    gi.append(_dot(ug, wx_ref[g]))
  nsplit = 4 if chain else 1
  prow = rows // nsplit
  pt = prow // 8
  sub = _iota((pt, 8, LANES), 1)
  row0 = _iota((prow, LANES), 0) == 0
  blocks = []
  for j in range(D_MODEL // LANES):
    cs = slice(j * LANES, (j + 1) * LANES)
    gs = slice((j % 2) * LANES, (j % 2 + 1) * LANES)
    hc = ch_scr[7:8, cs] if chain else None
    parts = []
    for part in range(nsplit):
      yield
      rs = slice(part * prow, (part + 1) * prow)
      gate_r = jax.nn.sigmoid(gr[j // 2][rs, gs] + ba_ref[:, cs])
      gate_i = jax.nn.sigmoid(gi[j // 2][rs, gs] + bx_ref[:, cs])
      log_a = (-LRU_C * gate_r) * _softplus(ap_ref[:, cs])
      a = jnp.exp(log_a)
      mult = jnp.sqrt(_neg_expm1_2x(log_a, a))
      if chain and part == 0:
        mult = jnp.where(jnp.logical_and(row0, c == 0), 1.0, mult)
      xin = u[rs, cs] * gate_i * mult

      a3 = a.reshape(pt, 8, LANES)
      b3 = xin.reshape(pt, 8, LANES)
      for s in (1, 2, 4):
        m = sub >= s
        b3 = jnp.where(m, a3 * pltpu.roll(b3, s, 1) + b3, b3)
        a3 = jnp.where(m, a3 * pltpu.roll(a3, s, 1), a3)
      if chain:
        hs = []
        for i in range(pt):
          hi = a3[i] * hc + b3[i]
          hs.append(hi)
          hc = hi[7:8, :]
        h = jnp.concatenate(hs, axis=0)
        if part == nsplit - 1:
          ch_scr[:, cs] = hs[-1]
          hl_ref[0, :, cs] = hc
      else:
        h3 = a3 * h0_ref[:, :, cs] + b3
        hl_ref[:, :, cs] = h3[:, 7:8, :]
        h = h3.reshape(prow, LANES)
      parts.append(jax.nn.sigmoid(ga_ref[rs, cs]) * (h * _gelu_tanh(y_ref[rs, cs])))
    blocks.append(_cat(parts, 0))
  return jnp.concatenate(blocks, axis=1)


def _lru_step(xy, gate_a, prev, h0, prm, l, *, nseq, tlen, nb=16):
  rows = nb * tlen
  row = lambda col: pl.BlockSpec((rows, D_MODEL), lambda b: (b, col))
  seq = pl.BlockSpec((nb, 1, D_MODEL), lambda b: (b, 0, 0))
  vec = pl.BlockSpec((None, 1, D_MODEL), lambda b: (l, 0, 0))
  gates_w = pl.BlockSpec((None, N_GROUPS, GROUP, GROUP), lambda b: (l, 0, 0, 0))
  return pl.pallas_call(
      _lru_step_kernel,
      grid=(nseq // nb,),
      in_specs=[row(0), row(1), row(0), row(0), seq,
                pl.BlockSpec((None, 4, D_MODEL), lambda b: (l, 0, 0)), vec, gates_w, vec, gates_w, vec, vec],
      out_specs=[row(0), seq],
      out_shape=[jax.ShapeDtypeStruct((nseq * tlen, D_MODEL), F32), jax.ShapeDtypeStruct((nseq, 1, D_MODEL), F32)],
      compiler_params=_params(("arbitrary",)),
      name="lru_step",
  )(xy, xy, gate_a, prev, h0, prm["conv_w"], prm["conv_b"], prm["wa_bd"], prm["lru_ba"], prm["wx_bd"],
    prm["lru_bx"], prm["lru_a_param"])


def _wkv_masks(tlen):
  lt = tlen.bit_length() - 1
  ri = _iota((CHUNK, GROUP), 0)
  cm = _iota((CHUNK, GROUP), 1) & (CHUNK - 1)
  same = (ri >> lt) == (cm >> lt)
  lane = _iota((CHUNK, LANES), 1)
  half = [jnp.where(lane < HEAD, 1.0, 0.0), jnp.where(lane >= HEAD, 1.0, 0.0)]
  return dict(
      strict=jnp.logical_and(same, cm < ri),
      incl=jnp.logical_and(same, cm <= ri),
      eye=jnp.where(cm == ri, 1.0, 0.0),
      half=half,
      half_b=[h.astype(BF16) for h in half],
      zero_b=jnp.zeros((CHUNK, LANES), BF16),
      ones_bd=_ones_bd(),
      colseq=(_iota((GROUP, 2 * CHUNK), 1) & (CHUNK - 1)) >> lt,
      lt=lt,
  )


def _bd_from_tiles(tiles, mk):
  z = mk["zero_b"]
  return jnp.concatenate([
      jnp.concatenate([tiles[0], z], axis=1), jnp.concatenate([tiles[1], z], axis=1),
      jnp.concatenate([z, tiles[2]], axis=1), jnp.concatenate([z, tiles[3]], axis=1)], axis=0)


def _bd(x, mk):
  xb = x.astype(BF16)
  return _bd_from_tiles([xb[:, (i // 2) * LANES:(i // 2 + 1) * LANES] * mk["half_b"][i % 2] for i in range(4)],
                        mk)


def _wkv_kernel(chain, tlen, cps, *refs):
  if chain:
    ops_ref, dt_ref, gg_ref, gbon_ref = refs[:4]
    lru_in = refs[4:14]
    gnw_ref, gnb_ref, out_ref, sout_ref, hl_ref, ct_ref, s_scr, cx_scr, ch_scr = refs[14:]
    lru = _Staged(_lru_body(True, *lru_in, hl_ref, ct_ref, cx_scr, ch_scr))
    lru.tick()
    tick = lru.tick
  else:
    tick = lambda: None
    ops_ref, dt_ref, pa_ref, gg_ref, gbon_ref, s0_ref, gnw_ref, gnb_ref, out_ref, sout_ref = refs
    pa = pa_ref[...]
  nsq = CHUNK // tlen
  mk = _wkv_masks(tlen)
  groups = range(N_GROUPS)
  units = [(ci, gi) for ci in range(cps) for gi in groups]
  cut = lambda ref, un: ref[un[0] * CHUNK:(un[0] + 1) * CHUNK, _gsl(un[1])]
  op = lambda idx, un: ops_ref[idx, un[0] * CHUNK:(un[0] + 1) * CHUNK, _gsl(un[1])]
  tile_of = lambda i: slice((i // 2) * LANES, (i // 2 + 1) * LANES)
  rblk = lambda i: slice(i * HEAD, (i + 1) * HEAD)

  if chain:
    c = pl.program_id(1)

    @pl.when(c == 0)
    def _():
      s_scr[...] = jnp.zeros_like(s_scr)

  lhs, n_ab, a_ak, a_rbk = {}, {}, {}, {}
  for idx, un in enumerate(units):
    lhs[un] = jnp.concatenate([op(0, un), op(1, un)], axis=0)
    gb = _dot(lhs[un], _bd(op(2, un), mk), trans_b=True)
    gk = _dot(lhs[un], _bd(op(3, un), mk), trans_b=True)
    n_ab[un] = jnp.where(mk["strict"], gb[:CHUNK], 0.0)
    a_ak[un] = jnp.where(mk["strict"], gk[:CHUNK], 0.0)
    a_rbk[un] = jnp.concatenate([jnp.where(mk["incl"], gb[CHUNK:], 0.0),
                                 jnp.where(mk["incl"], gk[CHUNK:], 0.0)], axis=1)
    if idx % 2:
      tick()

  t = {un: mk["eye"] + n_ab[un] for un in units}
  q = {un: _dot(n_ab[un], _bd(n_ab[un], mk)) for un in units}
  tick()
  levels = mk["lt"]
  for lvl in range(1, levels):
    for idx, un in enumerate(units):
      bq = _bd(q[un], mk)
      if lvl + 1 < levels:
        tq = _dot(jnp.concatenate([t[un], q[un]], axis=0), bq)
        t[un] = t[un] + tq[:CHUNK]
        q[un] = tq[CHUNK:]
      else:
        t[un] = t[un] + _dot(t[un], bq)
      if idx % 2:
        tick()

  bdv, a_v = {}, {}
  for idx, un in enumerate(units):
    bdv[un] = _bd(op(6, un), mk)
    a_v[un] = _dot(a_ak[un], bdv[un])
    if idx % 2:
      tick()
  lane_half = lambda i: slice((i % 2) * HEAD, (i % 2 + 1) * HEAD)
  o, gn = {}, {}

  def group_norm_staged(og, gi):
    o_mu = _dot(og, mk["ones_bd"]) * (1.0 / HEAD)
    yield
    oc = og - o_mu
    o_var = _dot(oc * oc, mk["ones_bd"]) * (1.0 / HEAD)
    return oc * lax.rsqrt(o_var + GN_EPS) * gnw_ref[:, _gsl(gi)] + gnb_ref[:, _gsl(gi)]

  def group_norm(og, gi):
    return _Staged(group_norm_staged(og, gi)).finish()

  if chain:
    xt, dcol = {}, {}
    for un in units:
      dt = cut(dt_ref, un)
      xpt = jnp.concatenate([op(4, un).astype(F32), op(5, un).astype(F32), dt, dt], axis=0).T
      xt[un] = xpt[:, :2 * CHUNK]
      dcol[un] = xpt[:, 2 * CHUNK:]
    p_cur = {gi: [s_scr[gi, i] for i in range(4)] for gi in groups}
    for ci in range(cps):
      a_p, r_p, u = {}, {}, {}
      for gi in groups:
        lp = _dot(lhs[(ci, gi)], _bd_from_tiles([tl.astype(BF16) for tl in p_cur[gi]], mk))
        a_p[gi], r_p[gi] = lp[:CHUNK], lp[CHUNK:]
      tick()
      if ci > 0:
        pend = [_Staged(group_norm_staged(o[(ci - 1, gi)], gi)) for gi in groups]
        for st in pend:
          st.tick()
      for gi in groups:
        u[gi] = _dot(t[(ci, gi)], _bd(a_p[gi] + a_v[(ci, gi)], mk))
      if ci > 0:
        for gi, st in zip(groups, pend):
          gn[(ci - 1, gi)] = st.finish()
      tick()
      for gi in groups:
        un = (ci, gi)
        y = jnp.concatenate([u[gi].astype(BF16), op(6, un)], axis=0)
        upd = _dot(xt[un], y)
        p_cur[gi] = [(p_cur[gi][i] * dcol[un][rblk(i)] + upd[rblk(i), tile_of(i)]) * mk["half"][i % 2]
                     for i in range(4)]
      tick()
      for gi in groups:
        un = (ci, gi)
        o[un] = r_p[gi] + _dot(a_rbk[un], jnp.concatenate([_bd(u[gi], mk), bdv[un]], axis=0))
      tick()
    for gi in groups:
      gn[(cps - 1, gi)] = group_norm(o[(cps - 1, gi)], gi)
    for gi in groups:
      for i in range(4):
        s_scr[gi, i] = p_cur[gi][i]

    @pl.when(c == pl.num_programs(1) - 1)
    def _():
      for gi in groups:
        for i in range(4):
          sout_ref[0, 4 * gi + i] = s_scr[gi, i][:, lane_half(i)].T
  else:
    zero = jnp.zeros((HEAD, HEAD), F32)
    tiles, a_p, r_p, u = {}, {}, {}, {}
    for un in units:
      ci, gi = un
      tiles[un], aps, rps = [], [], []
      for j in range(nsq):
        st = [s0_ref[ci * nsq + j, 4 * gi + i] for i in range(4)]
        tiles[un].append([jnp.concatenate([st[i], zero] if i % 2 == 0 else [zero, st[i]], axis=1)
                          for i in range(4)])
        lj = jnp.concatenate([lhs[un][j * tlen:(j + 1) * tlen],
                              lhs[un][CHUNK + j * tlen:CHUNK + (j + 1) * tlen]], axis=0)
        lpj = _dot(lj, _bd_from_tiles([tl.astype(BF16) for tl in tiles[un][j]], mk), trans_b=True)
        aps.append(lpj[:tlen])
        rps.append(lpj[tlen:])
      a_p[un] = jnp.concatenate(aps, axis=0)
      r_p[un] = jnp.concatenate(rps, axis=0)
    for un in units:
      u[un] = _dot(t[un], _bd(a_p[un] + a_v[un], mk))
    for un in units:
      o[un] = r_p[un] + _dot(a_rbk[un], jnp.concatenate([_bd(u[un], mk), bdv[un]], axis=0))
    for un in units:
      ci, gi = un
      yt = jnp.concatenate([u[un], op(6, un).astype(F32)], axis=0).T
      x = jnp.concatenate([op(4, un), op(5, un)], axis=0)
      dt = cut(dt_ref, un)
      for j in range(nsq):
        upd = _dot(jnp.where(mk["colseq"] == j, yt, 0.0), x)
        for i in range(4):
          new = tiles[un][j][i] * dt[j * tlen:j * tlen + 1, tile_of(i)] + upd[rblk(i), tile_of(i)]
          sout_ref[ci * nsq + j, 4 * gi + i] = new[:, lane_half(i)]

  if chain:
    outs = [_cat([gn[(ci, gi)] for ci in range(cps)], 0) for gi in groups]
    pa = lru.finish()
  else:
    outs = [group_norm(_cat([o[(ci, gi)] for ci in range(cps)], 0), gi) for gi in groups]
  out_ref[...] = pa + gbon_ref[...] + gg_ref[...] * jnp.concatenate(outs, axis=1)


N_HEADS = 4 * N_GROUPS


def _wkv_chain(ops, dtot, gg, gbon, xy, gate_a, prm, l, *, nseq, tlen, cps=2):
  rows = cps * CHUNK
  nstep = tlen // rows
  rmap = lambda b, c: (b * nstep + c, 0)
  row = pl.BlockSpec((rows, D_MODEL), rmap)
  vec = pl.BlockSpec((None, 1, D_MODEL), lambda b, c: (l, 0, 0))
  gates_w = pl.BlockSpec((None, N_GROUPS, GROUP, GROUP), lambda b, c: (l, 0, 0, 0))
  per_seq = lambda r, w: pl.BlockSpec((1, r, w), lambda b, c: (b, 0, 0))
  in_specs = [pl.BlockSpec((N_WKV_OPS, rows, D_MODEL), lambda b, c: (0, b * nstep + c, 0)), row, row, row,
              row, pl.BlockSpec((rows, D_MODEL), lambda b, c: (b * nstep + c, 1)), row,
              pl.BlockSpec((None, 4, D_MODEL), lambda b, c: (l, 0, 0)), vec, gates_w, vec, gates_w, vec, vec,
              vec, vec]
  args = [ops, dtot, gg, gbon, xy, xy, gate_a,
          prm["conv_w"], prm["conv_b"], prm["wa_bd"], prm["lru_ba"], prm["wx_bd"], prm["lru_bx"],
          prm["lru_a_param"], prm["gn_w"], prm["gn_b"]]
  return pl.pallas_call(
      functools.partial(_wkv_kernel, True, CHUNK, cps),
      grid=(nseq, nstep), in_specs=in_specs,
      out_specs=[row, pl.BlockSpec((1, N_HEADS, HEAD, HEAD), lambda b, c: (b, 0, 0, 0)),
                 per_seq(1, D_MODEL), per_seq(8, D_MODEL)],
      out_shape=[jax.ShapeDtypeStruct((nseq * tlen, D_MODEL), F32),
                 jax.ShapeDtypeStruct((nseq, N_HEADS, HEAD, HEAD), F32),
                 jax.ShapeDtypeStruct((nseq, 1, D_MODEL), F32), jax.ShapeDtypeStruct((nseq, 8, D_MODEL), F32)],
      scratch_shapes=[pltpu.VMEM((N_GROUPS, 4, HEAD, LANES), F32), pltpu.VMEM((8, D_MODEL), F32),
                      pltpu.VMEM((8, D_MODEL), F32)],
      compiler_params=_params(("arbitrary", "arbitrary")),
      name="wkv_chain",
  )(*args)


def _wkv_step(ops, dtot, pa, gg, gbon, states, prm, l, *, nseq, tlen, cps=2):
  rows = cps * CHUNK
  sblk = rows // tlen
  row = pl.BlockSpec((rows, D_MODEL), lambda b, c: (b, 0))
  vec = pl.BlockSpec((None, 1, D_MODEL), lambda b, c: (l, 0, 0))
  sspec = pl.BlockSpec((None, sblk, N_HEADS, HEAD, HEAD), lambda b, c: (l, b, 0, 0, 0))
  in_specs = [pl.BlockSpec((N_WKV_OPS, rows, D_MODEL), lambda b, c: (0, b, 0)), row, row, row, row, sspec,
              vec, vec]
  return pl.pallas_call(
      functools.partial(_wkv_kernel, False, tlen, cps),
      grid=(nseq // sblk, 1), in_specs=in_specs,
      out_specs=[row, sspec],
      out_shape=[jax.ShapeDtypeStruct((nseq * tlen, D_MODEL), F32),
                 jax.ShapeDtypeStruct(states.shape, F32)],
      input_output_aliases={5: 1},
      compiler_params=_params(("arbitrary", "arbitrary")),
      name="wkv_step",
  )(ops, dtot, pa, gg, gbon, states, prm["gn_w"], prm["gn_b"])


def _mlp_kernel(x_ref, m_ref, wo_ref, g1_ref, b1_ref, w1_ref, w2_ref, g2_ref, b2_ref,
                o_ref, x1_scr, x1b_scr, acc_scr):
  j = pl.program_id(1)
  last = pl.num_programs(1) - 1
  sub = 256
  nb = x_ref.shape[0] // sub
  rb = lambda b: slice(b * sub, (b + 1) * sub)

  def outproj(b):
    return DN_ALPHA * x_ref[rb(b), :] + _dot(m_ref[rb(b), :], wo_ref[...])

  def ln1(b, y):
    x1 = _layer_norm(y, g1_ref[...], b1_ref[...])
    x1_scr[rb(b), :] = x1
    x1b_scr[rb(b), :] = x1.astype(BF16)

  def ff(b, first):
    hid = jnp.square(jnp.maximum(_dot(x1b_scr[rb(b), :], w1_ref[...]), 0.0))
    part = _dot(hid, w2_ref[...])
    acc_scr[rb(b), :] = part if first else acc_scr[rb(b), :] + part

  def ln2(b):
    o_ref[rb(b), :] = _layer_norm(DN_ALPHA * x1_scr[rb(b), :] + acc_scr[rb(b), :], g2_ref[...], b2_ref[...])

  @pl.when(j == 0)
  def _():
    y = outproj(0)
    for b in range(nb):
      y_next = outproj(b + 1) if b + 1 < nb else None
      ln1(b, y)
      if b > 0:
        ff(b - 1, True)
      y = y_next
    ff(nb - 1, True)

  @pl.when(jnp.logical_and(j > 0, j < last))
  def _():
    for b in range(nb):
      ff(b, False)

  @pl.when(j == last)
  def _():
    for b in range(nb):
      ff(b, False)
      if b > 0:
        ln2(b - 1)
    ln2(nb - 1)


def _mlp(x, merged, prm, l, tm=512, tf=1024):
  n = x.shape[0]
  vec = pl.BlockSpec((None, 1, D_MODEL), lambda i, j: (l, 0, 0))
  row = pl.BlockSpec((tm, D_MODEL), lambda i, j: (i, 0))
  return pl.pallas_call(
      _mlp_kernel,
      grid=(n // tm, D_FF // tf),
      in_specs=[
          row, row,
          pl.BlockSpec((None, D_MODEL, D_MODEL), lambda i, j: (l, 0, 0)),
          vec, vec,
          pl.BlockSpec((None, D_MODEL, tf), lambda i, j: (l, 0, j)),
          pl.BlockSpec((None, tf, D_MODEL), lambda i, j: (l, j, 0)),
          vec, vec,
      ],
      out_specs=row,
      out_shape=jax.ShapeDtypeStruct((n, D_MODEL), F32),
      scratch_shapes=[pltpu.VMEM((tm, D_MODEL), F32), pltpu.VMEM((tm, D_MODEL), BF16),
                      pltpu.VMEM((tm, D_MODEL), F32)],
      compiler_params=_params(("arbitrary", "arbitrary")),
      name="outproj_mlp",
  )(x, merged, prm["w_out"], prm["ln1_g"], prm["ln1_b"], prm["mlp_w1"], prm["mlp_w2"],
    prm["ln2_g"], prm["ln2_b"])


def _block_diag4(w):
  depth = w.shape[0]
  w5 = w.reshape(depth, N_GROUPS, 4, HEAD, HEAD)
  eye = jnp.eye(4, dtype=w.dtype)
  return jnp.einsum("lgaij,ab->lgaibj", w5, eye).reshape(depth, N_GROUPS, GROUP, GROUP)


def kernel(x_prompt, x_sample, state_conv, state_lru, state_shift, state_wkv, w_in, conv_w, conv_b, lru_wa,
           lru_ba, lru_wx, lru_bx, lru_a_param, shift_mu, decay_up, w0, aaa_up, a0, gate_up, k_k, k_a, r_k,
           gn_w, gn_b, w_out, ln1_g, ln1_b, mlp_w1, mlp_w2, ln2_g, ln2_b):
  bp, tp, _ = x_prompt.shape
  bs, ts, _ = x_sample.shape
  depth = w_in.shape[0]

  row = lambda p: p.reshape(depth, 1, -1)
  w_in_b = w_in.astype(BF16)
  zero = jnp.zeros((depth, 64, D_MODEL), F32)
  lora_w = jnp.concatenate([jnp.concatenate([decay_up, zero], axis=2),
                            jnp.concatenate([zero, aaa_up], axis=2)], axis=1).astype(BF16)
  prm = dict(
      conv_w=conv_w, conv_b=row(conv_b),
      wa_bd=_block_diag4(lru_wa).astype(BF16), wx_bd=_block_diag4(lru_wx).astype(BF16),
      lru_ba=row(lru_ba), lru_bx=row(lru_bx), lru_a_param=row(lru_a_param),
      shift_mu=row(shift_mu), lora_w=lora_w, w0=row(w0), a0=row(a0), gate_up=gate_up.astype(BF16),
      k_k=row(k_k), k_a=row(k_a), r_k=row(r_k), gn_w=row(gn_w), gn_b=row(gn_b),
      w_out=w_out.astype(BF16), ln1_g=row(ln1_g), ln1_b=row(ln1_b),
      mlp_w1=mlp_w1.astype(BF16), mlp_w2=mlp_w2.astype(BF16), ln2_g=row(ln2_g), ln2_b=row(ln2_b),
  )

  xp = x_prompt.reshape(bp * tp, D_MODEL)
  xs = x_sample.reshape(bs * ts, D_MODEL)
  conv_p, lru_p, shift_p, wkv_p = [], [], [], []
  conv_s, lru_s, shift_s = [], [], []
  wkv_s = state_wkv
  for l in range(depth):
    xy, ga, ops, dtot, gg, gbon, stail = _front(xp, w_in_b, prm, l, chain=True, nseq=bp, tlen=tp)
    mg, z, hl, ctail = _wkv_chain(ops, dtot, gg, gbon, xy, ga, prm, l, nseq=bp, tlen=tp)
    xp = _mlp(xp, mg, prm, l, tm=1024)
    conv_p.append(ctail[:, 5:, :])
    lru_p.append(hl.reshape(bp, D_MODEL))
    shift_p.append(stail[:, 7, :])
    wkv_p.append(z)

    xy, ga, ops, dtot, gg, gbon, sh = _front(xs, w_in_b, prm, l, chain=False, nseq=bs, tlen=ts, tm=128,
                                             sprev=jnp.repeat(state_shift[l], ts, axis=0))
    prev = jnp.pad(state_conv[l], ((0, 0), (5, 0), (0, 0))).reshape(bs * 8, D_MODEL)
    pa, hl = _lru_step(xy, ga, prev, state_lru[l].reshape(bs, 1, D_MODEL), prm, l, nseq=bs, tlen=ts)
    mg, wkv_s = _wkv_step(ops, dtot, pa, gg, gbon, wkv_s, prm, l, nseq=bs, tlen=ts)
    xs = _mlp(xs, mg, prm, l, tm=1024)
    conv_s.append(xy.reshape(bs, ts, XY_WIDTH)[:, ts - 3:, :D_MODEL])
    lru_s.append(hl.reshape(bs, D_MODEL))
    shift_s.append(sh.reshape(bs, ts, SHIFT_WIDTH)[:, ts - 1])

  return (xp.reshape(bp, tp, D_MODEL), xs.reshape(bs, ts, D_MODEL),
          jnp.stack(conv_p), jnp.stack(lru_p), jnp.stack(shift_p), jnp.stack(wkv_p),
          jnp.stack(conv_s), jnp.stack(lru_s), jnp.stack(shift_s), wkv_s)
```

```python
import functools
import math

import jax
import jax.numpy as jnp
from jax import lax
from jax.experimental import pallas as pl
from jax.experimental.pallas import tpu as pltpu

F32 = jnp.float32
BF16 = jnp.bfloat16

D_MODEL = 1024
DEPTH = 4
LRU_C = 8.0
HEAD = 64
GROUP = 256
LANES = 128
N_GROUPS = D_MODEL // GROUP
SHIFT_WIDTH = 3 * D_MODEL + 64 + 64 + 128
XY_WIDTH = 2 * D_MODEL
IN_COLS = 2 * XY_WIDTH + SHIFT_WIDTH
D_FF = 4 * D_MODEL
DN_ALPHA = (2 * DEPTH) ** 0.25
LN_EPS = 1e-5
GN_EPS = 64e-5
CHUNK = 64
VMEM_LIMIT = 56 * 1024 * 1024
NP_CUM = 2
N_WKV_OPS = 7


def _split(x, n):
  pieces = []
  rem = x
  for i in range(n):
    p = rem.astype(BF16)
    pieces.append(p)
    if i + 1 < n:
      rem = rem - p.astype(F32)
  return pieces


def _dot(a, b, trans_b=False):
  dims = (((1,), (1,)), ((), ())) if trans_b else (((1,), (0,)), ((), ()))
  return lax.dot_general(a.astype(BF16), b.astype(BF16), dims, preferred_element_type=F32)


def _softplus(x):
  return jnp.maximum(x, 0.0) + jnp.log1p(jnp.exp(-jnp.abs(x)))


def _neg_expm1_2x(x, ex):
  return -jnp.tanh(x) * (ex * ex + 1.0)


def _gelu_tanh(x):
  c = math.sqrt(2.0 / math.pi)
  return x * (0.5 * (1.0 + jnp.tanh(c * (x + 0.044715 * (x * x * x)))))


def _layer_norm(x, g, b):
  mu = jnp.mean(x, axis=-1, keepdims=True)
  xc = x - mu
  var = jnp.mean(xc * xc, axis=-1, keepdims=True)
  return xc * lax.rsqrt(var + LN_EPS) * g + b


def _iota(shape, dim):
  return lax.broadcasted_iota(jnp.int32, shape, dim)


def _cat(xs, axis):
  return xs[0] if len(xs) == 1 else jnp.concatenate(xs, axis=axis)


def _params(sem):
  return pltpu.CompilerParams(dimension_semantics=sem, vmem_limit_bytes=VMEM_LIMIT)


def _ones_bd():
  same = (_iota((GROUP, GROUP), 0) >> 6) == (_iota((GROUP, GROUP), 1) >> 6)
  return jnp.where(same, 1.0, 0.0).astype(BF16)


def _gsl(gi):
  return slice(gi * GROUP, (gi + 1) * GROUP)


def _front_dots(x_ref, w_ref, xy_ref, ga_ref):
  xb = x_ref[...].astype(BF16)
  sh = _dot(xb, w_ref[:, XY_WIDTH:XY_WIDTH + SHIFT_WIDTH])
  xy_ref[...] = _dot(xb, w_ref[:, 0:XY_WIDTH])
  gt = _dot(xb, w_ref[:, XY_WIDTH + SHIFT_WIDTH:IN_COLS])
  ga_ref[...] = gt[:, :D_MODEL]
  return sh, gt[:, D_MODEL:]


def _front_prep(sh, gate_b, prev, tlen, prm_refs, out_refs, r0=0):
  mu_ref, lw_ref, w0_ref, a0_ref, gup_ref, kk_ref, ka_ref, rk_ref = prm_refs
  ops_ref, dt_ref, gg_ref, gbon_ref = out_refs
  rows = sh.shape[0]
  rsl = slice(r0, r0 + rows)
  nchunk = rows // CHUNK
  lt = tlen.bit_length() - 1
  ones_bd = _ones_bd()
  headsum = lambda z: _cat([_dot(z[:, _gsl(gi)], ones_bd) for gi in range(N_GROUPS)], 1)

  mixed = sh + (prev - sh) * mu_ref[...]
  r = mixed[:, 0:D_MODEL]
  k = mixed[:, D_MODEL:2 * D_MODEL]
  v = mixed[:, 2 * D_MODEL:3 * D_MODEL]
  xl = mixed[:, 3 * D_MODEL:3 * D_MODEL + 128]
  xg = mixed[:, 3 * D_MODEL + 128:SHIFT_WIDTH]
  lane = _iota((rows, 128), 1)
  lo = _dot(jnp.where(lane < 64, jnp.tanh(xl), xl), lw_ref[...])
  g = _dot(jax.nn.sigmoid(xg), gup_ref[...])
  kk = k * kk_ref[...]
  ss = headsum(kk * kk)
  yield

  w_log = -_softplus(-(w0_ref[...] + lo[:, :D_MODEL])) - 0.5
  w = -jnp.exp(w_log)
  a = jax.nn.sigmoid(a0_ref[...] + lo[:, D_MODEL:])
  k_h = k * (1.0 + (a - 1.0) * ka_ref[...])
  rkr = r * k_h * rk_ref[...]
  bonus_sum = headsum(rkr)
  rs = _iota((2 * CHUNK, CHUNK), 0)
  cs = _iota((2 * CHUNK, CHUNK), 1)
  rr = rs & (CHUNK - 1)
  cum_mask = jnp.logical_and((rr >> lt) == (cs >> lt), jnp.logical_or(cs <= rr, rs >= CHUNK))
  cum_mask = jnp.where(cum_mask, 1.0, 0.0).astype(BF16)
  one_seq = tlen == CHUNK
  if one_seq:
    cum_mask = cum_mask[:CHUNK]
  cums, tots = [], []
  for ci in range(nchunk):
    pieces = _split(w[ci * CHUNK:(ci + 1) * CHUNK], NP_CUM)
    cw = _dot(cum_mask, pieces[0])
    for p in pieces[1:]:
      cw = cw + _dot(cum_mask, p)
    cums.append(cw[:CHUNK])
    tots.append(jnp.broadcast_to(cw[CHUNK - 1:CHUNK], (CHUNK, D_MODEL)) if one_seq else cw[CHUNK:])
  yield

  kkn = kk / jnp.maximum(jnp.sqrt(ss), 1e-12)
  bv = kkn * a
  gg = jax.nn.sigmoid(gate_b) * g
  gg_ref[rsl, :] = gg
  gbon_ref[rsl, :] = gg * (bonus_sum * v)
  yield
  cum = _cat(cums, 0)
  tot = _cat(tots, 0)
  e_neg = jnp.exp(-cum)
  e_rem = jnp.exp(tot - cum)
  ops_ref[0, rsl, :] = (-kkn * jnp.exp(cum - w)).astype(BF16)
  ops_ref[1, rsl, :] = (r * jnp.exp(cum)).astype(BF16)
  ops_ref[2, rsl, :] = (bv * e_neg).astype(BF16)
  ops_ref[3, rsl, :] = (k_h * e_neg).astype(BF16)
  yield
  ops_ref[4, rsl, :] = (bv * e_rem).astype(BF16)
  ops_ref[5, rsl, :] = (k_h * e_rem).astype(BF16)
  ops_ref[6, rsl, :] = v.astype(BF16)
  dt_ref[rsl, :] = jnp.exp(tot)


def _front_prep_fine(mix, gate_b, prm_refs, out_refs, r0):
  mu_ref, lw_ref, w0_ref, a0_ref, gup_ref, kk_ref, ka_ref, rk_ref = prm_refs
  ops_ref, dt_ref, gg_ref, gbon_ref = out_refs
  rsl = slice(r0, r0 + CHUNK)
  ones_bd = _ones_bd()
  cum_mask = jnp.where(_iota((CHUNK, CHUNK), 1) <= _iota((CHUNK, CHUNK), 0), 1.0, 0.0).astype(BF16)
  lane = _iota((CHUNK, LANES), 1)
  col = lambda base, gi: slice(base + gi * GROUP, base + (gi + 1) * GROUP)

  xl = mix(slice(3 * D_MODEL, 3 * D_MODEL + LANES))
  xg = mix(slice(3 * D_MODEL + LANES, SHIFT_WIDTH))
  lo = _dot(jnp.where(lane < 64, jnp.tanh(xl), xl), lw_ref[...])
  g = _dot(jax.nn.sigmoid(xg), gup_ref[...])
  yield
  st = {}
  for gi in range(N_GROUPS):
    k = mix(col(D_MODEL, gi))
    kk = k * kk_ref[:, _gsl(gi)]
    st[gi] = dict(k=k, kk=kk, ss=_dot(kk * kk, ones_bd))
    yield
  for gi in range(N_GROUPS):
    cs, d = _gsl(gi), st[gi]
    w = -jnp.exp(-_softplus(-(w0_ref[:, cs] + lo[:, cs])) - 0.5)
    a = jax.nn.sigmoid(a0_ref[:, cs] + lo[:, col(D_MODEL, gi)])
    k_h = d["k"] * (1.0 + (a - 1.0) * ka_ref[:, cs])
    r = mix(col(0, gi))
    bsum = _dot(r * k_h * rk_ref[:, cs], ones_bd)
    pieces = _split(w, NP_CUM)
    cum = _dot(cum_mask, pieces[0])
    for p in pieces[1:]:
      cum = cum + _dot(cum_mask, p)
    d.update(w=w, a=a, k_h=k_h, r=r, bsum=bsum, cum=cum)
    yield
  for gi in range(N_GROUPS):
    cs, d = _gsl(gi), st[gi]
    v = mix(col(2 * D_MODEL, gi))
    kkn = d["kk"] / jnp.maximum(jnp.sqrt(d["ss"]), 1e-12)
    bv = kkn * d["a"]
    gg = jax.nn.sigmoid(gate_b(cs)) * g[:, cs]
    gg_ref[rsl, cs] = gg
    gbon_ref[rsl, cs] = gg * (d["bsum"] * v)
    cum = d["cum"]
    tot = jnp.broadcast_to(cum[CHUNK - 1:CHUNK], cum.shape)
    e_neg = jnp.exp(-cum)
    e_rem = jnp.exp(tot - cum)
    ops_ref[0, rsl, cs] = (-kkn * jnp.exp(cum - d["w"])).astype(BF16)
    ops_ref[1, rsl, cs] = (d["r"] * jnp.exp(cum)).astype(BF16)
    ops_ref[2, rsl, cs] = (bv * e_neg).astype(BF16)
    ops_ref[3, rsl, cs] = (d["k_h"] * e_neg).astype(BF16)
    ops_ref[4, rsl, cs] = (bv * e_rem).astype(BF16)
    ops_ref[5, rsl, cs] = (d["k_h"] * e_rem).astype(BF16)
    ops_ref[6, rsl, cs] = v.astype(BF16)
    dt_ref[rsl, cs] = jnp.exp(tot)
    yield


def _front_step_kernel(tlen, x_ref, w_ref, sp_ref, *refs):
  prm_refs, (xy_ref, ga_ref), out_refs, sh_ref = refs[:8], refs[8:10], refs[10:14], refs[14]
  sh, gate_b = _front_dots(x_ref, w_ref, xy_ref, ga_ref)
  sh_ref[...] = sh
  row = _iota(sh.shape, 0)
  prev = jnp.where((row & (tlen - 1)) == 0, sp_ref[...], pltpu.roll(sh, 1, 0))
  for _ in _front_prep(sh, gate_b, prev, tlen, prm_refs, out_refs):
    pass


def _front_chain_kernel(nstep, ntiles, x_ref, w_ref, *refs):
  prm_refs, (xy_ref, ga_ref), out_refs = refs[:8], refs[8:10], refs[10:14]
  st_ref, sh_a, sh_b, gb_a, gb_b, carry_scr = refs[14:]
  s = pl.program_id(0)
  rows = x_ref.shape[0]
  bufs = ((sh_a, gb_a), (sh_b, gb_b))

  nchunk = rows // CHUNK
  step_w = GROUP
  pieces = [("xy", c, c + step_w) for c in range(0, XY_WIDTH, step_w)]
  pieces += [("sh", c, min(c + step_w, SHIFT_WIDTH)) for c in range(0, SHIFT_WIDTH, step_w)]
  pieces += [("ga", c, c + step_w) for c in range(0, D_MODEL, step_w)]
  pieces += [("gb", c, c + step_w) for c in range(0, D_MODEL, step_w)]
  w_base = {"xy": 0, "sh": XY_WIDTH, "ga": XY_WIDTH + SHIFT_WIDTH, "gb": XY_WIDTH + SHIFT_WIDTH + D_MODEL}

  def dot_piece(parity, piece):
    sh_scr, gb_scr = bufs[parity]
    dst, c0, c1 = piece
    res = _dot(x_ref[...].astype(BF16), w_ref[:, w_base[dst] + c0:w_base[dst] + c1])
    {"xy": xy_ref, "sh": sh_scr, "ga": ga_ref, "gb": gb_scr}[dst][:, c0:c1] = res

  def prep_piece(parity, ci):
    sh_scr, gb_scr = bufs[parity]
    r0 = ci * CHUNK
    row0 = _iota((CHUNK, LANES), 0) == 0
    seq_start = ((s - 1) % nstep) == 0

    def mix(cs):
      sh = sh_scr[r0:r0 + CHUNK, cs]
      first = jnp.where(seq_start, 0.0, carry_scr[7:8, cs]) if ci == 0 else sh_scr[r0 - 1:r0, cs]
      width = sh.shape[1]
      prev = jnp.where(jnp.concatenate([row0] * (width // LANES), axis=1), first, pltpu.roll(sh, 1, 0))
      return sh + (prev - sh) * mu_ref[:, cs]

    return _front_prep_fine(mix, lambda cs: gb_scr[r0:r0 + CHUNK, cs], prm_refs, out_refs, r0)

  def prep_all(parity, between=lambda done, total: None):
    total = nchunk * (1 + 3 * N_GROUPS)
    done = 0
    for ci in range(nchunk):
      for _ in prep_piece(parity, ci):
        done += 1
        between(done, total)
    tail = bufs[parity][0][rows - 8:rows, :]
    carry_scr[...] = tail
    st_ref[0] = tail

  mu_ref = prm_refs[0]

  @pl.when(s == 0)
  def _():
    for piece in pieces:
      dot_piece(0, piece)

  for parity in (0, 1):
    @pl.when(jnp.logical_and(jnp.logical_and(s > 0, s < ntiles), s % 2 == parity))
    def _(parity=parity):
      todo = list(pieces)

      def between(done, total):
        while todo and (len(pieces) - len(todo)) * total < done * len(pieces):
          dot_piece(parity, todo.pop(0))

      prep_all(1 - parity, between)
      for piece in todo:
        dot_piece(parity, piece)

  @pl.when(s == ntiles)
  def _():
    prep_all((ntiles - 1) % 2)


def _front(x, w_in_b, prm, l, *, chain, nseq, tlen, tm=256, sprev=None):
  n = x.shape[0]
  ntiles = n // tm
  if chain:
    nstep = tlen // tm
    grid = (ntiles + 1,)
    cur = lambda s: jnp.minimum(s, ntiles - 1)
    old = lambda s: jnp.maximum(s - 1, 0)
    body = functools.partial(_front_chain_kernel, nstep, ntiles)
  else:
    grid = (ntiles,)
    cur = old = lambda s: s
    body = functools.partial(_front_step_kernel, tlen)
  row = lambda width, which: pl.BlockSpec((tm, width), lambda s: (which(s), 0))
  vec = pl.BlockSpec((None, 1, D_MODEL), lambda s: (l, 0, 0))
  in_specs = [row(D_MODEL, cur), pl.BlockSpec((None, D_MODEL, IN_COLS), lambda s: (l, 0, 0),
                                              pipeline_mode=pl.Buffered(1))]
  args = [x, w_in_b]
  if not chain:
    in_specs.append(row(SHIFT_WIDTH, cur))
    args.append(sprev)
  in_specs += [pl.BlockSpec((None, 1, SHIFT_WIDTH), lambda s: (l, 0, 0)),
               pl.BlockSpec((None, 128, 2 * D_MODEL), lambda s: (l, 0, 0)),
               vec, vec,
               pl.BlockSpec((None, 128, D_MODEL), lambda s: (l, 0, 0)),
               vec, vec, vec]
  args += [prm["shift_mu"], prm["lora_w"], prm["w0"], prm["a0"], prm["gate_up"], prm["k_k"], prm["k_a"],
           prm["r_k"]]
  out_specs = [row(XY_WIDTH, cur), row(D_MODEL, cur),
               pl.BlockSpec((N_WKV_OPS, tm, D_MODEL), lambda s: (0, old(s), 0)),
               row(D_MODEL, old), row(D_MODEL, old), row(D_MODEL, old)]
  f32 = lambda width: jax.ShapeDtypeStruct((n, width), F32)
  out_shape = [f32(XY_WIDTH), f32(D_MODEL), jax.ShapeDtypeStruct((N_WKV_OPS, n, D_MODEL), BF16),
               f32(D_MODEL), f32(D_MODEL), f32(D_MODEL)]
  scratch = []
  if chain:
    out_specs.append(pl.BlockSpec((1, 8, SHIFT_WIDTH), lambda s: (old(s) // nstep, 0, 0)))
    out_shape.append(jax.ShapeDtypeStruct((nseq, 8, SHIFT_WIDTH), F32))
    scratch = [pltpu.VMEM((tm, SHIFT_WIDTH), F32), pltpu.VMEM((tm, SHIFT_WIDTH), F32),
               pltpu.VMEM((tm, D_MODEL), F32), pltpu.VMEM((tm, D_MODEL), F32),
               pltpu.VMEM((8, SHIFT_WIDTH), F32)]
  else:
    out_specs.append(row(SHIFT_WIDTH, cur))
    out_shape.append(f32(SHIFT_WIDTH))
  return pl.pallas_call(
      body, grid=grid, in_specs=in_specs, out_specs=out_specs, out_shape=out_shape, scratch_shapes=scratch,
      compiler_params=_params(("arbitrary",)),
      name="front_chain" if chain else "front_step",
  )(*args)


class _Staged:
  def __init__(self, gen):
    self.gen, self.value, self.done = gen, None, False

  def tick(self):
    if not self.done:
      try:
        next(self.gen)
      except StopIteration as stop:
        self.value, self.done = stop.value, True

  def finish(self):
    while not self.done:
      self.tick()
    return self.value


def _lru_step_kernel(*refs):
  n_in = 12
  refs[n_in][...] = _Staged(_lru_body(False, *refs[:n_in], *refs[n_in + 1:])).finish()


def _lru_body(chain, *refs):
  if chain:
    (x_ref, y_ref, ga_ref, cw_ref, cb_ref, wa_ref, ba_ref, wx_ref, bx_ref, ap_ref,
     hl_ref, ct_ref, cx_scr, ch_scr) = refs
  else:
    (x_ref, y_ref, ga_ref, prev_ref, h0_ref, cw_ref, cb_ref, wa_ref, ba_ref, wx_ref, bx_ref,
     ap_ref, hl_ref) = refs
  rows = x_ref.shape[0]
  nt = rows // 8
  x = x_ref[...]
  x3 = x.reshape(nt, 8, D_MODEL)
  if chain:
    c = pl.program_id(1)

    @pl.when(c == 0)
    def _():
      cx_scr[...] = jnp.zeros_like(cx_scr)
      ch_scr[...] = jnp.zeros_like(ch_scr)

    prev3 = jnp.concatenate([cx_scr[...].reshape(1, 8, D_MODEL), x3[:nt - 1]], axis=0)
    cx_scr[...] = x_ref[rows - 8:rows, :]
    ct_ref[0] = x_ref[rows - 8:rows, :]
  else:
    prev3 = prev_ref[...].reshape(nt, 8, D_MODEL)

  sub = _iota((nt, 8, D_MODEL), 1)
  u = cb_ref[...] + cw_ref[3:4, :] * x
  for j in (1, 2, 3):
    xs = pltpu.roll(jnp.where(sub >= 8 - j, prev3, x3), j, 1)
    u = u + cw_ref[3 - j:4 - j, :] * xs.reshape(rows, D_MODEL)

  gr, gi = [], []
  for g in range(N_GROUPS):
    ug = u[:, _gsl(g)].astype(BF16)
    gr.append(_dot(ug, wa_ref[g]))
    gi.append(_dot(ug, wx_ref[g]))
  nsplit = 8 if chain else 1
  prow = rows // nsplit
  pt = prow // 8
  sub = _iota((pt, 8, LANES), 1)
  row0 = _iota((prow, LANES), 0) == 0
  blocks = []
  for j in range(D_MODEL // LANES):
    cs = slice(j * LANES, (j + 1) * LANES)
    gs = slice((j % 2) * LANES, (j % 2 + 1) * LANES)
    hc = ch_scr[7:8, cs] if chain else None
    parts = []
    for part in range(nsplit):
      yield
      rs = slice(part * prow, (part + 1) * prow)
      gate_r = jax.nn.sigmoid(gr[j // 2][rs, gs] + ba_ref[:, cs])
      gate_i = jax.nn.sigmoid(gi[j // 2][rs, gs] + bx_ref[:, cs])
      log_a = (-LRU_C * gate_r) * _softplus(ap_ref[:, cs])
      a = jnp.exp(log_a)
      mult = jnp.sqrt(_neg_expm1_2x(log_a, a))
      if chain and part == 0:
        mult = jnp.where(jnp.logical_and(row0, c == 0), 1.0, mult)
      xin = u[rs, cs] * gate_i * mult

      a3 = a.reshape(pt, 8, LANES)
      b3 = xin.reshape(pt, 8, LANES)
      for s in (1, 2, 4):
        m = sub >= s
        b3 = jnp.where(m, a3 * pltpu.roll(b3, s, 1) + b3, b3)
        a3 = jnp.where(m, a3 * pltpu.roll(a3, s, 1), a3)
      if chain:
        hs = []
        for i in range(pt):
          hi = a3[i] * hc + b3[i]
          hs.append(hi)
          hc = hi[7:8, :]
        h = jnp.concatenate(hs, axis=0)
        if part == nsplit - 1:
          ch_scr[:, cs] = hs[-1]
          hl_ref[0, :, cs] = hc
      else:
        h3 = a3 * h0_ref[:, :, cs] + b3
        hl_ref[:, :, cs] = h3[:, 7:8, :]
        h = h3.reshape(prow, LANES)
      parts.append(jax.nn.sigmoid(ga_ref[rs, cs]) * (h * _gelu_tanh(y_ref[rs, cs])))
    blocks.append(_cat(parts, 0))
  return jnp.concatenate(blocks, axis=1)


def _lru_step(xy, gate_a, prev, h0, prm, l, *, nseq, tlen, nb=16):
  rows = nb * tlen
  row = lambda col: pl.BlockSpec((rows, D_MODEL), lambda b: (b, col))
  seq = pl.BlockSpec((nb, 1, D_MODEL), lambda b: (b, 0, 0))
  vec = pl.BlockSpec((None, 1, D_MODEL), lambda b: (l, 0, 0))
  gates_w = pl.BlockSpec((None, N_GROUPS, GROUP, GROUP), lambda b: (l, 0, 0, 0))
  return pl.pallas_call(
      _lru_step_kernel,
      grid=(nseq // nb,),
      in_specs=[row(0), row(1), row(0), row(0), seq,
                pl.BlockSpec((None, 4, D_MODEL), lambda b: (l, 0, 0)), vec, gates_w, vec, gates_w, vec, vec],
      out_specs=[row(0), seq],
      out_shape=[jax.ShapeDtypeStruct((nseq * tlen, D_MODEL), F32), jax.ShapeDtypeStruct((nseq, 1, D_MODEL), F32)],
      compiler_params=_params(("arbitrary",)),
      name="lru_step",
  )(xy, xy, gate_a, prev, h0, prm["conv_w"], prm["conv_b"], prm["wa_bd"], prm["lru_ba"], prm["wx_bd"],
    prm["lru_bx"], prm["lru_a_param"])


def _wkv_masks(tlen):
  lt = tlen.bit_length() - 1
  ri = _iota((CHUNK, GROUP), 0)
  cm = _iota((CHUNK, GROUP), 1) & (CHUNK - 1)
  same = (ri >> lt) == (cm >> lt)
  lane = _iota((CHUNK, LANES), 1)
  half = [jnp.where(lane < HEAD, 1.0, 0.0), jnp.where(lane >= HEAD, 1.0, 0.0)]
  return dict(
      strict=jnp.logical_and(same, cm < ri),
      incl=jnp.logical_and(same, cm <= ri),
      eye=jnp.where(cm == ri, 1.0, 0.0),
      half=half,
      half_b=[h.astype(BF16) for h in half],
      zero_b=jnp.zeros((CHUNK, LANES), BF16),
      ones_bd=_ones_bd(),
      colseq=(_iota((GROUP, 2 * CHUNK), 1) & (CHUNK - 1)) >> lt,
      lt=lt,
  )


def _bd_from_tiles(tiles, mk):
  z = mk["zero_b"]
  return jnp.concatenate([
      jnp.concatenate([tiles[0], z], axis=1), jnp.concatenate([tiles[1], z], axis=1),
      jnp.concatenate([z, tiles[2]], axis=1), jnp.concatenate([z, tiles[3]], axis=1)], axis=0)


def _bd(x, mk):
  xb = x.astype(BF16)
  return _bd_from_tiles([xb[:, (i // 2) * LANES:(i // 2 + 1) * LANES] * mk["half_b"][i % 2] for i in range(4)],
                        mk)


def _wkv_kernel(chain, tlen, cps, *refs):
  if chain:
    ops_ref, dt_ref, gg_ref, gbon_ref = refs[:4]
    lru_in = refs[4:14]
    gnw_ref, gnb_ref, out_ref, sout_ref, hl_ref, ct_ref, s_scr, cx_scr, ch_scr = refs[14:]
    lru = _Staged(_lru_body(True, *lru_in, hl_ref, ct_ref, cx_scr, ch_scr))
    lru.tick()
    tick = lru.tick
  else:
    tick = lambda: None
    ops_ref, dt_ref, pa_ref, gg_ref, gbon_ref, s0_ref, gnw_ref, gnb_ref, out_ref, sout_ref = refs
    pa = pa_ref[...]
  nsq = CHUNK // tlen
  mk = _wkv_masks(tlen)
  groups = range(N_GROUPS)
  units = [(ci, gi) for ci in range(cps) for gi in groups]
  cut = lambda ref, un: ref[un[0] * CHUNK:(un[0] + 1) * CHUNK, _gsl(un[1])]
  op = lambda idx, un: ops_ref[idx, un[0] * CHUNK:(un[0] + 1) * CHUNK, _gsl(un[1])]
  tile_of = lambda i: slice((i // 2) * LANES, (i // 2 + 1) * LANES)
  rblk = lambda i: slice(i * HEAD, (i + 1) * HEAD)

  if chain:
    c = pl.program_id(1)

    @pl.when(c == 0)
    def _():
      s_scr[...] = jnp.zeros_like(s_scr)

  lhs, n_ab, a_ak, a_rbk = {}, {}, {}, {}
  for idx, un in enumerate(units):
    lhs[un] = jnp.concatenate([op(0, un), op(1, un)], axis=0)
    gb = _dot(lhs[un], _bd(op(2, un), mk), trans_b=True)
    gk = _dot(lhs[un], _bd(op(3, un), mk), trans_b=True)
    n_ab[un] = jnp.where(mk["strict"], gb[:CHUNK], 0.0)
    a_ak[un] = jnp.where(mk["strict"], gk[:CHUNK], 0.0)
    a_rbk[un] = jnp.concatenate([jnp.where(mk["incl"], gb[CHUNK:], 0.0),
                                 jnp.where(mk["incl"], gk[CHUNK:], 0.0)], axis=1)
    tick()

  t = {un: mk["eye"] + n_ab[un] for un in units}
  q = {un: _dot(n_ab[un], _bd(n_ab[un], mk)) for un in units}
  tick()
  levels = mk["lt"]
  for lvl in range(1, levels):
    for idx, un in enumerate(units):
      bq = _bd(q[un], mk)
      if lvl + 1 < levels:
        tq = _dot(jnp.concatenate([t[un], q[un]], axis=0), bq)
        t[un] = t[un] + tq[:CHUNK]
        q[un] = tq[CHUNK:]
      else:
        t[un] = t[un] + _dot(t[un], bq)
      tick()

  bdv, a_v = {}, {}
  for idx, un in enumerate(units):
    bdv[un] = _bd(op(6, un), mk)
    a_v[un] = _dot(a_ak[un], bdv[un])
    tick()
  lane_half = lambda i: slice((i % 2) * HEAD, (i % 2 + 1) * HEAD)
  o, gn = {}, {}

  def group_norm_staged(og, gi):
    o_mu = _dot(og, mk["ones_bd"]) * (1.0 / HEAD)
    yield
    oc = og - o_mu
    o_var = _dot(oc * oc, mk["ones_bd"]) * (1.0 / HEAD)
    return oc * lax.rsqrt(o_var + GN_EPS) * gnw_ref[:, _gsl(gi)] + gnb_ref[:, _gsl(gi)]

  def group_norm(og, gi):
    return _Staged(group_norm_staged(og, gi)).finish()

  if chain:
    xt, dcol = {}, {}
    for un in units:
      dt = cut(dt_ref, un)
      xpt = jnp.concatenate([op(4, un).astype(F32), op(5, un).astype(F32), dt, dt], axis=0).T
      xt[un] = xpt[:, :2 * CHUNK]
      dcol[un] = xpt[:, 2 * CHUNK:]
    p_cur = {gi: [s_scr[gi, i] for i in range(4)] for gi in groups}
    for ci in range(cps):
      a_p, r_p, u = {}, {}, {}
      for gi in groups:
        lp = _dot(lhs[(ci, gi)], _bd_from_tiles([tl.astype(BF16) for tl in p_cur[gi]], mk))
        a_p[gi], r_p[gi] = lp[:CHUNK], lp[CHUNK:]
      tick()
      if ci > 0:
        pend = [_Staged(group_norm_staged(o[(ci - 1, gi)], gi)) for gi in groups]
        for st in pend:
          st.tick()
      for gi in groups:
        u[gi] = _dot(t[(ci, gi)], _bd(a_p[gi] + a_v[(ci, gi)], mk))
      if ci > 0:
        for gi, st in zip(groups, pend):
          gn[(ci - 1, gi)] = st.finish()
      tick()
      for gi in groups:
        un = (ci, gi)
        y = jnp.concatenate([u[gi].astype(BF16), op(6, un)], axis=0)
        upd = _dot(xt[un], y)
        p_cur[gi] = [(p_cur[gi][i] * dcol[un][rblk(i)] + upd[rblk(i), tile_of(i)]) * mk["half"][i % 2]
                     for i in range(4)]
      tick()
      for gi in groups:
        un = (ci, gi)
        o[un] = r_p[gi] + _dot(a_rbk[un], jnp.concatenate([_bd(u[gi], mk), bdv[un]], axis=0))
      tick()
    for gi in groups:
      gn[(cps - 1, gi)] = group_norm(o[(cps - 1, gi)], gi)
    for gi in groups:
      for i in range(4):
        s_scr[gi, i] = p_cur[gi][i]

    @pl.when(c == pl.num_programs(1) - 1)
    def _():
      for gi in groups:
        for i in range(4):
          sout_ref[0, 4 * gi + i] = s_scr[gi, i][:, lane_half(i)].T
  else:
    zero = jnp.zeros((HEAD, HEAD), F32)
    tiles, a_p, r_p, u = {}, {}, {}, {}
    for un in units:
      ci, gi = un
      tiles[un], aps, rps = [], [], []
      for j in range(nsq):
        st = [s0_ref[ci * nsq + j, 4 * gi + i] for i in range(4)]
        tiles[un].append([jnp.concatenate([st[i], zero] if i % 2 == 0 else [zero, st[i]], axis=1)
                          for i in range(4)])
        lj = jnp.concatenate([lhs[un][j * tlen:(j + 1) * tlen],
                              lhs[un][CHUNK + j * tlen:CHUNK + (j + 1) * tlen]], axis=0)
        lpj = _dot(lj, _bd_from_tiles([tl.astype(BF16) for tl in tiles[un][j]], mk), trans_b=True)
        aps.append(lpj[:tlen])
        rps.append(lpj[tlen:])
      a_p[un] = jnp.concatenate(aps, axis=0)
      r_p[un] = jnp.concatenate(rps, axis=0)
    for un in units:
      u[un] = _dot(t[un], _bd(a_p[un] + a_v[un], mk))
    for un in units:
      o[un] = r_p[un] + _dot(a_rbk[un], jnp.concatenate([_bd(u[un], mk), bdv[un]], axis=0))
    for un in units:
      ci, gi = un
      yt = jnp.concatenate([u[un], op(6, un).astype(F32)], axis=0).T
      x = jnp.concatenate([op(4, un), op(5, un)], axis=0)
      dt = cut(dt_ref, un)
      for j in range(nsq):
        upd = _dot(jnp.where(mk["colseq"] == j, yt, 0.0), x)
        for i in range(4):
          new = tiles[un][j][i] * dt[j * tlen:j * tlen + 1, tile_of(i)] + upd[rblk(i), tile_of(i)]
          sout_ref[ci * nsq + j, 4 * gi + i] = new[:, lane_half(i)]

  if chain:
    outs = [_cat([gn[(ci, gi)] for ci in range(cps)], 0) for gi in groups]
    pa = lru.finish()
  else:
    outs = [group_norm(_cat([o[(ci, gi)] for ci in range(cps)], 0), gi) for gi in groups]
  out_ref[...] = pa + gbon_ref[...] + gg_ref[...] * jnp.concatenate(outs, axis=1)


N_HEADS = 4 * N_GROUPS


def _wkv_chain(ops, dtot, gg, gbon, xy, gate_a, prm, l, *, nseq, tlen, cps=2):
  rows = cps * CHUNK
  nstep = tlen // rows
  rmap = lambda b, c: (b * nstep + c, 0)
  row = pl.BlockSpec((rows, D_MODEL), rmap)
  vec = pl.BlockSpec((None, 1, D_MODEL), lambda b, c: (l, 0, 0))
  gates_w = pl.BlockSpec((None, N_GROUPS, GROUP, GROUP), lambda b, c: (l, 0, 0, 0))
  per_seq = lambda r, w: pl.BlockSpec((1, r, w), lambda b, c: (b, 0, 0))
  in_specs = [pl.BlockSpec((N_WKV_OPS, rows, D_MODEL), lambda b, c: (0, b * nstep + c, 0)), row, row, row,
              row, pl.BlockSpec((rows, D_MODEL), lambda b, c: (b * nstep + c, 1)), row,
              pl.BlockSpec((None, 4, D_MODEL), lambda b, c: (l, 0, 0)), vec, gates_w, vec, gates_w, vec, vec,
              vec, vec]
  args = [ops, dtot, gg, gbon, xy, xy, gate_a,
          prm["conv_w"], prm["conv_b"], prm["wa_bd"], prm["lru_ba"], prm["wx_bd"], prm["lru_bx"],
          prm["lru_a_param"], prm["gn_w"], prm["gn_b"]]
  return pl.pallas_call(
      functools.partial(_wkv_kernel, True, CHUNK, cps),
      grid=(nseq, nstep), in_specs=in_specs,
      out_specs=[row, pl.BlockSpec((1, N_HEADS, HEAD, HEAD), lambda b, c: (b, 0, 0, 0)),
                 per_seq(1, D_MODEL), per_seq(8, D_MODEL)],
      out_shape=[jax.ShapeDtypeStruct((nseq * tlen, D_MODEL), F32),
                 jax.ShapeDtypeStruct((nseq, N_HEADS, HEAD, HEAD), F32),
                 jax.ShapeDtypeStruct((nseq, 1, D_MODEL), F32), jax.ShapeDtypeStruct((nseq, 8, D_MODEL), F32)],
      scratch_shapes=[pltpu.VMEM((N_GROUPS, 4, HEAD, LANES), F32), pltpu.VMEM((8, D_MODEL), F32),
                      pltpu.VMEM((8, D_MODEL), F32)],
      compiler_params=_params(("arbitrary", "arbitrary")),
      name="wkv_chain",
  )(*args)


def _wkv_step(ops, dtot, pa, gg, gbon, states, prm, l, *, nseq, tlen, cps=2):
  rows = cps * CHUNK
  sblk = rows // tlen
  row = pl.BlockSpec((rows, D_MODEL), lambda b, c: (b, 0))
  vec = pl.BlockSpec((None, 1, D_MODEL), lambda b, c: (l, 0, 0))
  sspec = pl.BlockSpec((None, sblk, N_HEADS, HEAD, HEAD), lambda b, c: (l, b, 0, 0, 0))
  in_specs = [pl.BlockSpec((N_WKV_OPS, rows, D_MODEL), lambda b, c: (0, b, 0)), row, row, row, row, sspec,
              vec, vec]
  return pl.pallas_call(
      functools.partial(_wkv_kernel, False, tlen, cps),
      grid=(nseq // sblk, 1), in_specs=in_specs,
      out_specs=[row, sspec],
      out_shape=[jax.ShapeDtypeStruct((nseq * tlen, D_MODEL), F32),
                 jax.ShapeDtypeStruct(states.shape, F32)],
      input_output_aliases={5: 1},
      compiler_params=_params(("arbitrary", "arbitrary")),
      name="wkv_step",
  )(ops, dtot, pa, gg, gbon, states, prm["gn_w"], prm["gn_b"])


def _mlp_kernel(x_ref, m_ref, wo_ref, g1_ref, b1_ref, w1_ref, w2_ref, g2_ref, b2_ref,
                o_ref, x1_scr, x1b_scr, acc_scr):
  j = pl.program_id(1)
  last = pl.num_programs(1) - 1
  sub = 256
  nb = x_ref.shape[0] // sub
  rb = lambda b: slice(b * sub, (b + 1) * sub)

  def outproj(b):
    return DN_ALPHA * x_ref[rb(b), :] + _dot(m_ref[rb(b), :], wo_ref[...])

  def ln1(b, y):
    x1 = _layer_norm(y, g1_ref[...], b1_ref[...])
    x1_scr[rb(b), :] = x1
    x1b_scr[rb(b), :] = x1.astype(BF16)

  def ff(b, first):
    hid = jnp.square(jnp.maximum(_dot(x1b_scr[rb(b), :], w1_ref[...]), 0.0))
    part = _dot(hid, w2_ref[...])
    acc_scr[rb(b), :] = part if first else acc_scr[rb(b), :] + part

  def ln2(b):
    o_ref[rb(b), :] = _layer_norm(DN_ALPHA * x1_scr[rb(b), :] + acc_scr[rb(b), :], g2_ref[...], b2_ref[...])

  @pl.when(j == 0)
  def _():
    y = outproj(0)
    for b in range(nb):
      y_next = outproj(b + 1) if b + 1 < nb else None
      ln1(b, y)
      if b > 0:
        ff(b - 1, True)
      y = y_next
    ff(nb - 1, True)

  @pl.when(jnp.logical_and(j > 0, j < last))
  def _():
    for b in range(nb):
      ff(b, False)

  @pl.when(j == last)
  def _():
    for b in range(nb):
      ff(b, False)
      if b > 0:
        ln2(b - 1)
    ln2(nb - 1)


def _mlp(x, merged, prm, l, tm=512, tf=1024):
  n = x.shape[0]
  vec = pl.BlockSpec((None, 1, D_MODEL), lambda i, j: (l, 0, 0))
  row = pl.BlockSpec((tm, D_MODEL), lambda i, j: (i, 0))
  return pl.pallas_call(
      _mlp_kernel,
      grid=(n // tm, D_FF // tf),
      in_specs=[
          row, row,
          pl.BlockSpec((None, D_MODEL, D_MODEL), lambda i, j: (l, 0, 0)),
          vec, vec,
          pl.BlockSpec((None, D_MODEL, tf), lambda i, j: (l, 0, j)),
          pl.BlockSpec((None, tf, D_MODEL), lambda i, j: (l, j, 0)),
          vec, vec,
      ],
      out_specs=row,
      out_shape=jax.ShapeDtypeStruct((n, D_MODEL), F32),
      scratch_shapes=[pltpu.VMEM((tm, D_MODEL), F32), pltpu.VMEM((tm, D_MODEL), BF16),
                      pltpu.VMEM((tm, D_MODEL), F32)],
      compiler_params=_params(("arbitrary", "arbitrary")),
      name="outproj_mlp",
  )(x, merged, prm["w_out"], prm["ln1_g"], prm["ln1_b"], prm["mlp_w1"], prm["mlp_w2"],
    prm["ln2_g"], prm["ln2_b"])


def _block_diag4(w):
  depth = w.shape[0]
  w5 = w.reshape(depth, N_GROUPS, 4, HEAD, HEAD)
  eye = jnp.eye(4, dtype=w.dtype)
  return jnp.einsum("lgaij,ab->lgaibj", w5, eye).reshape(depth, N_GROUPS, GROUP, GROUP)


def kernel(x_prompt, x_sample, state_conv, state_lru, state_shift, state_wkv, w_in, conv_w, conv_b, lru_wa,
           lru_ba, lru_wx, lru_bx, lru_a_param, shift_mu, decay_up, w0, aaa_up, a0, gate_up, k_k, k_a, r_k,
           gn_w, gn_b, w_out, ln1_g, ln1_b, mlp_w1, mlp_w2, ln2_g, ln2_b):
  bp, tp, _ = x_prompt.shape
  bs, ts, _ = x_sample.shape
  depth = w_in.shape[0]

  row = lambda p: p.reshape(depth, 1, -1)
  w_in_b = w_in.astype(BF16)
  zero = jnp.zeros((depth, 64, D_MODEL), F32)
  lora_w = jnp.concatenate([jnp.concatenate([decay_up, zero], axis=2),
                            jnp.concatenate([zero, aaa_up], axis=2)], axis=1).astype(BF16)
  prm = dict(
      conv_w=conv_w, conv_b=row(conv_b),
      wa_bd=_block_diag4(lru_wa).astype(BF16), wx_bd=_block_diag4(lru_wx).astype(BF16),
      lru_ba=row(lru_ba), lru_bx=row(lru_bx), lru_a_param=row(lru_a_param),
      shift_mu=row(shift_mu), lora_w=lora_w, w0=row(w0), a0=row(a0), gate_up=gate_up.astype(BF16),
      k_k=row(k_k), k_a=row(k_a), r_k=row(r_k), gn_w=row(gn_w), gn_b=row(gn_b),
      w_out=w_out.astype(BF16), ln1_g=row(ln1_g), ln1_b=row(ln1_b),
      mlp_w1=mlp_w1.astype(BF16), mlp_w2=mlp_w2.astype(BF16), ln2_g=row(ln2_g), ln2_b=row(ln2_b),
  )

  xp = x_prompt.reshape(bp * tp, D_MODEL)
  xs = x_sample.reshape(bs * ts, D_MODEL)
  conv_p, lru_p, shift_p, wkv_p = [], [], [], []
  conv_s, lru_s, shift_s = [], [], []
  wkv_s = state_wkv
  for l in range(depth):
    xy, ga, ops, dtot, gg, gbon, stail = _front(xp, w_in_b, prm, l, chain=True, nseq=bp, tlen=tp)
    mg, z, hl, ctail = _wkv_chain(ops, dtot, gg, gbon, xy, ga, prm, l, nseq=bp, tlen=tp)
    xp = _mlp(xp, mg, prm, l, tm=1024)
    conv_p.append(ctail[:, 5:, :])
    lru_p.append(hl.reshape(bp, D_MODEL))
    shift_p.append(stail[:, 7, :])
    wkv_p.append(z)

    xy, ga, ops, dtot, gg, gbon, sh = _front(xs, w_in_b, prm, l, chain=False, nseq=bs, tlen=ts, tm=128,
                                             sprev=jnp.repeat(state_shift[l], ts, axis=0))
    prev = jnp.pad(state_conv[l], ((0, 0), (5, 0), (0, 0))).reshape(bs * 8, D_MODEL)
    pa, hl = _lru_step(xy, ga, prev, state_lru[l].reshape(bs, 1, D_MODEL), prm, l, nseq=bs, tlen=ts)
    mg, wkv_s = _wkv_step(ops, dtot, pa, gg, gbon, wkv_s, prm, l, nseq=bs, tlen=ts)
    xs = _mlp(xs, mg, prm, l, tm=1024)
    conv_s.append(xy.reshape(bs, ts, XY_WIDTH)[:, ts - 3:, :D_MODEL])
    lru_s.append(hl.reshape(bs, D_MODEL))
    shift_s.append(sh.reshape(bs, ts, SHIFT_WIDTH)[:, ts - 1])

  return (xp.reshape(bp, tp, D_MODEL), xs.reshape(bs, ts, D_MODEL),
          jnp.stack(conv_p), jnp.stack(lru_p), jnp.stack(shift_p), jnp.stack(wkv_p),
          jnp.stack(conv_s), jnp.stack(lru_s), jnp.stack(shift_s), wkv_s)
```
